```python
import jax
import jax.numpy as jnp
from jax import lax
import numpy as np

D_MODEL = 1024
BATCH = 8
SEQ = 4096
DEPTH = 1
DEC_BATCH = 8
DEC_SEQ = 64
PAST_LEN = 2048

CHUNK = 64
LEFT_CHUNKS = 8
BAND = 512
N_HEADS_SB = 8
N_HEADS_CB = 8
HEAD_DIM = 64
WIDTH_SB = 512
WIDTH_CB = 512
REL_CLIP = 128
Q_BLOCK = 128
EPS = 1e-6
NEG_INF = -1e30
ATTN_SCALE = 0.125
IN_COLS = 6144
SPLIT_POINTS = (512, 1024, 1536, 2048, 2560, 3072, 3584, 4096, 5120)

kernel_name = "hybrid_stickbreak_chunkband_stream_step"


def rms_norm(x, w):
    xf = x.astype(jnp.float32)
    ms = jnp.mean(xf * xf, axis=-1, keepdims=True)
    return (xf * lax.rsqrt(ms + EPS)).astype(x.dtype) * w


def project(x, norm_w, w_in, q_norm_w, k_norm_w):
    h = rms_norm(x, norm_w) @ w_in
    sb_q, sb_k, sb_v, sb_z, cb_q, cb_k, cb_v, cb_z, g_sb, g_cb = jnp.split(h, SPLIT_POINTS, axis=-1)
    lead = x.shape[:-1]
    sb_q = sb_q.reshape(lead + (N_HEADS_SB, HEAD_DIM))
    sb_k = sb_k.reshape(lead + (N_HEADS_SB, HEAD_DIM))
    sb_v = sb_v.reshape(lead + (N_HEADS_SB, HEAD_DIM))
    cb_q = rms_norm(cb_q.reshape(lead + (N_HEADS_CB, HEAD_DIM)), q_norm_w)
    cb_k = rms_norm(cb_k.reshape(lead + (N_HEADS_CB, HEAD_DIM)), k_norm_w)
    cb_v = cb_v.reshape(lead + (N_HEADS_CB, HEAD_DIM))
    return sb_q, sb_k, sb_v, sb_z, cb_q, cb_k, cb_v, cb_z, g_sb, g_cb


def stick_breaking(q, k, v, q_pos, k_pos):
    z = jnp.einsum("bqhd,bkhd->bhqk", q, k).astype(jnp.float32) * ATTN_SCALE
    causal = k_pos[None, :] < q_pos[:, None]
    log_beta = jax.nn.log_sigmoid(z)
    log_keep = jnp.where(causal, jax.nn.log_sigmoid(-z), 0.0)
    suffix = lax.cumsum(log_keep, axis=3, reverse=True) - log_keep
    w = jnp.where(causal, jnp.exp(log_beta + suffix), 0.0)
    return jnp.einsum("bhqk,bkhd->bqhd", w.astype(v.dtype), v)


def band_attention(q, k, v, q_pos, k_pos, rel_bias):
    s = jnp.einsum("...qhd,...khd->...hqk", q, k).astype(jnp.float32) * ATTN_SCALE
    rel = jnp.clip(q_pos[..., :, None] - k_pos[..., None, :], -REL_CLIP, REL_CLIP) + REL_CLIP
    bias = jnp.moveaxis(rel_bias[:, rel], 0, -3).astype(jnp.float32)
    mask = (k_pos >= 0)[..., None, None, :]
    s = jnp.where(mask, s + bias, NEG_INF)
    p = jax.nn.softmax(s, axis=-1).astype(v.dtype)
    return jnp.einsum("...hqk,...khd->...qhd", p, v)


def chunk_band(kv, nc):
    b = kv.shape[0]
    kc = kv.reshape((b, nc, CHUNK) + kv.shape[2:])
    kp = jnp.pad(kc, ((0, 0), (LEFT_CHUNKS, 0), (0, 0), (0, 0), (0, 0)))
    return jnp.concatenate([kp[:, o:o + nc] for o in range(LEFT_CHUNKS + 1)], axis=2)


def merge(x, o_sb, z_sb, o_cb, z_cb, g_sb, g_cb, w_proj_sb, w_proj_cb, w_out):
    lead = x.shape[:-1]
    b_sb = (o_sb.reshape(lead + (WIDTH_SB,)) * jax.nn.silu(z_sb)) @ w_proj_sb
    b_cb = (o_cb.reshape(lead + (WIDTH_CB,)) * jax.nn.silu(z_cb)) @ w_proj_cb
    h = jax.nn.sigmoid(g_sb) * b_sb + jax.nn.sigmoid(g_cb) * b_cb
    return x + h @ w_out


def setup_inputs(seed: int = 0) -> dict:
    key = jax.random.key(seed)
    ks = jax.random.split(key, 14)
    f = jnp.float32
    r = min(BAND, PAST_LEN)

    def nrm(k, shape, scale):
        return jax.random.normal(k, shape, f) * scale

    return {
        "x_prompt": nrm(ks[0], (BATCH, SEQ, D_MODEL), 1.0),
        "x_sample": nrm(ks[1], (DEC_BATCH, DEC_SEQ, D_MODEL), 1.0),
        "cache_sb_k": nrm(ks[2], (DEPTH, DEC_BATCH, PAST_LEN, N_HEADS_SB, HEAD_DIM), 1.0),
        "cache_sb_v": nrm(ks[3], (DEPTH, DEC_BATCH, PAST_LEN, N_HEADS_SB, HEAD_DIM), 1.0),
        "cache_cb_k": nrm(ks[4], (DEPTH, DEC_BATCH, r, N_HEADS_CB, HEAD_DIM), 1.0),
        "cache_cb_v": nrm(ks[5], (DEPTH, DEC_BATCH, r, N_HEADS_CB, HEAD_DIM), 1.0),
        "norm_w": 1.0 + nrm(ks[6], (DEPTH, D_MODEL), 0.05),
        "w_in": nrm(ks[7], (DEPTH, D_MODEL, IN_COLS), D_MODEL ** -0.5),
        "q_norm_w": 1.0 + nrm(ks[8], (DEPTH, HEAD_DIM), 0.05),
        "k_norm_w": 1.0 + nrm(ks[9], (DEPTH, HEAD_DIM), 0.05),
        "rel_bias": nrm(ks[10], (DEPTH, N_HEADS_CB, 2 * REL_CLIP + 1), 0.5),
        "w_proj_sb": nrm(ks[11], (DEPTH, WIDTH_SB, D_MODEL), WIDTH_SB ** -0.5),
        "w_proj_cb": nrm(ks[12], (DEPTH, WIDTH_CB, D_MODEL), WIDTH_CB ** -0.5),
        "w_out": nrm(ks[13], (DEPTH, D_MODEL, D_MODEL), D_MODEL ** -0.5),
    }


def reference(x_prompt, x_sample, cache_sb_k, cache_sb_v, cache_cb_k, cache_cb_v,
              norm_w, w_in, q_norm_w, k_norm_w, rel_bias, w_proj_sb, w_proj_cb, w_out):
    b, t, _ = x_prompt.shape
    tn = x_sample.shape[1]
    p = cache_sb_k.shape[2]
    r = cache_cb_k.shape[2]
    nc = t // CHUNK
    nb = t // Q_BLOCK
    keep = min(BAND, t)

    pos_p = jnp.arange(t)
    qpos_blocks = pos_p.reshape(nb, Q_BLOCK)
    cb_qpos = pos_p.reshape(nc, CHUNK)
    cb_kpos = (jnp.arange(nc)[:, None] - LEFT_CHUNKS) * CHUNK + jnp.arange((LEFT_CHUNKS + 1) * CHUNK)[None, :]
    sb_s_kpos = jnp.arange(p + tn)
    s_qpos = p + jnp.arange(tn)
    cb_s_kpos = p - r + jnp.arange(r + tn)

    y_p, y_s = x_prompt, x_sample
    sb_k_p, sb_v_p, cb_k_p, cb_v_p = [], [], [], []
    sb_k_s, sb_v_s, cb_k_s, cb_v_s = [], [], [], []
    for l in range(DEPTH):
        sq, sk, sv, sz, cq, ck, cv, cz, gs, gc = project(y_p, norm_w[l], w_in[l], q_norm_w[l], k_norm_w[l])
        qb = jnp.moveaxis(sq.reshape(b, nb, Q_BLOCK, N_HEADS_SB, HEAD_DIM), 1, 0)
        o_sb = lax.map(lambda a: stick_breaking(a[0], sk, sv, a[1], pos_p), (qb, qpos_blocks))
        o_sb = jnp.moveaxis(o_sb, 0, 1).reshape(b, t, N_HEADS_SB, HEAD_DIM)
        o_cb = band_attention(cq.reshape(b, nc, CHUNK, N_HEADS_CB, HEAD_DIM),
                              chunk_band(ck, nc), chunk_band(cv, nc),
                              cb_qpos, cb_kpos, rel_bias[l])
        o_cb = o_cb.reshape(b, t, N_HEADS_CB, HEAD_DIM)
        y_p_next = merge(y_p, o_sb, sz, o_cb, cz, gs, gc, w_proj_sb[l], w_proj_cb[l], w_out[l])
        sb_k_p.append(sk)
        sb_v_p.append(sv)
        cb_k_p.append(ck[:, t - keep:])
        cb_v_p.append(cv[:, t - keep:])

        sq2, sk2, sv2, sz2, cq2, ck2, cv2, cz2, gs2, gc2 = project(y_s, norm_w[l], w_in[l], q_norm_w[l], k_norm_w[l])
        k_all = jnp.concatenate([cache_sb_k[l].astype(sk2.dtype), sk2], axis=1)
        v_all = jnp.concatenate([cache_sb_v[l].astype(sv2.dtype), sv2], axis=1)
        o_sb2 = stick_breaking(sq2, k_all, v_all, s_qpos, sb_s_kpos)
        kb = jnp.concatenate([cache_cb_k[l].astype(ck2.dtype), ck2], axis=1)
        vb = jnp.concatenate([cache_cb_v[l].astype(cv2.dtype), cv2], axis=1)
        o_cb2 = band_attention(cq2, kb, vb, s_qpos, cb_s_kpos, rel_bias[l])
        y_s_next = merge(y_s, o_sb2, sz2, o_cb2, cz2, gs2, gc2, w_proj_sb[l], w_proj_cb[l], w_out[l])
        sb_k_s.append(sk2)
        sb_v_s.append(sv2)
        cb_k_s.append(ck2)
        cb_v_s.append(cv2)

        y_p, y_s = y_p_next, y_s_next

    return (y_p, y_s,
            jnp.stack(sb_k_p), jnp.stack(sb_v_p), jnp.stack(cb_k_p), jnp.stack(cb_v_p),
            jnp.stack(sb_k_s), jnp.stack(sb_v_s), jnp.stack(cb_k_s), jnp.stack(cb_v_s))
```

```python
import functools

import jax
import jax.numpy as jnp
import numpy as np
from jax import lax
from jax.experimental import pallas as pl
from jax.experimental.pallas import tpu as pltpu

F32 = jnp.float32
BF16 = jnp.bfloat16

D_MODEL = 1024
CHUNK = 64
LEFT_CHUNKS = 8
BAND = LEFT_CHUNKS * CHUNK
N_HEADS = 8
HEAD_DIM = 64
WIDTH = N_HEADS * HEAD_DIM
REL_CLIP = 128
EPS = 1e-6
NEG_INF = -1e30
ATTN_SCALE = 0.125
IN_COLS = 4 * WIDTH + 4 * WIDTH + 2 * D_MODEL

LANES = 128
HEAD_PAIRS = N_HEADS // 2
EXP_ZERO_BELOW = -104.0
VMEM_LIMIT = 56 * 1024 * 1024


def _dot(a, b):
    return jnp.dot(a, b, preferred_element_type=F32)


def _dot_nt(a, b):
    return lax.dot_general(a, b, (((1,), (1,)), ((), ())), preferred_element_type=F32)


def _sigmoid(x):
    return 1.0 / (1.0 + jnp.exp(-x))


def _head_rms_norm(h, w128, lane_lo):
    sq = h * h
    s_lo = jnp.sum(jnp.where(lane_lo, sq, 0.0), axis=-1, keepdims=True)
    s_hi = jnp.sum(jnp.where(lane_lo, 0.0, sq), axis=-1, keepdims=True)
    r_lo = lax.rsqrt(s_lo * (1.0 / HEAD_DIM) + EPS)
    r_hi = lax.rsqrt(s_hi * (1.0 / HEAD_DIM) + EPS)
    return (h * jnp.where(lane_lo, r_lo, r_hi)) * w128


def _proj_kernel(x_ref, nw_ref, w_ref, qnw_ref, knw_ref,
                 sbq_ref, sbk32_ref, sbk16_ref, sbv32_ref, sbv16_ref, sbz_ref,
                 cbq_ref, cbk16_ref, cbv16_ref, cbk32_ref, cbv32_ref, cbz_ref,
                 gsb_ref, gcb_ref, *, pad_blocks, first_tail_step):
    j = pl.program_id(1)

    if pad_blocks:
        @pl.when(j < pad_blocks)
        def _():
            cbk16_ref[...] = jnp.zeros(cbk16_ref.shape, BF16)
            cbv16_ref[...] = jnp.zeros(cbv16_ref.shape, BF16)

    @pl.when(j >= pad_blocks)
    def _():
        x = x_ref[0]
        ms = jnp.mean(x * x, axis=-1, keepdims=True)
        xb = ((x * lax.rsqrt(ms + EPS)) * nw_ref[...]).astype(BF16)
        tm = x.shape[0]
        lane_lo = lax.broadcasted_iota(jnp.int32, (tm, LANES), 1) < HEAD_DIM

        def col(c0, width):
            return _dot(xb, w_ref[:, c0:c0 + width])

        sbq_ref[0] = (col(0, WIDTH) * ATTN_SCALE).astype(BF16)
        h = col(WIDTH, WIDTH)
        sbk32_ref[0] = h
        sbk16_ref[0] = h.astype(BF16)
        h = col(2 * WIDTH, WIDTH)
        sbv32_ref[0] = h
        sbv16_ref[0] = h.astype(BF16)
        sbz_ref[0] = col(3 * WIDTH, WIDTH)

        for s in range(HEAD_PAIRS):
            sl = slice(s * LANES, (s + 1) * LANES)
            hq = _head_rms_norm(col(4 * WIDTH + s * LANES, LANES), qnw_ref[...], lane_lo)
            cbq_ref[0, :, sl] = (hq * ATTN_SCALE).astype(BF16)
            hk = _head_rms_norm(col(5 * WIDTH + s * LANES, LANES), knw_ref[...], lane_lo)
            cbk16_ref[0, :, sl] = hk.astype(BF16)

            @pl.when(j >= first_tail_step)
            def _():
                cbk32_ref[0, :, sl] = hk

        h = col(6 * WIDTH, WIDTH)
        cbv16_ref[0] = h.astype(BF16)

        @pl.when(j >= first_tail_step)
        def _():
            cbv32_ref[0] = h

        cbz_ref[0] = col(7 * WIDTH, WIDTH)
        gsb_ref[0] = col(8 * WIDTH, D_MODEL)
        gcb_ref[0] = col(8 * WIDTH + D_MODEL, D_MODEL)


def _project(x, nw, w_bf, qnw128, knw128, *, tm, pad_rows):
    b, t, _ = x.shape
    nt = t // tm
    pad_blocks = pad_rows // tm
    tail = min(BAND, t)
    tail_blocks = tail // tm
    first_tail_step = pad_blocks + nt - tail_blocks

    def row(bi, j):
        return (bi, jnp.maximum(j - pad_blocks, 0), 0)

    def padded(bi, j):
        return (bi, j, 0)

    def tail_map(bi, j):
        return (bi, jnp.maximum(j - first_tail_step, 0), 0)

    def const(bi, j):
        return (0, 0)

    def out(width, dtype, rows=t):
        return jax.ShapeDtypeStruct((b, rows, width), dtype)

    blk = lambda width, imap: pl.BlockSpec((1, tm, width), imap)
    out_shape = [
        out(WIDTH, BF16), out(WIDTH, F32), out(WIDTH, BF16), out(WIDTH, F32), out(WIDTH, BF16),
        out(WIDTH, F32),
        out(WIDTH, BF16), out(WIDTH, BF16, t + pad_rows), out(WIDTH, BF16, t + pad_rows),
        out(WIDTH, F32, tail), out(WIDTH, F32, tail), out(WIDTH, F32),
        out(D_MODEL, F32), out(D_MODEL, F32),
    ]
    out_specs = [
        blk(WIDTH, row), blk(WIDTH, row), blk(WIDTH, row), blk(WIDTH, row), blk(WIDTH, row),
        blk(WIDTH, row),
        blk(WIDTH, row), blk(WIDTH, padded), blk(WIDTH, padded),
        blk(WIDTH, tail_map), blk(WIDTH, tail_map), blk(WIDTH, row),
        blk(D_MODEL, row), blk(D_MODEL, row),
    ]
    in_specs = [
        blk(D_MODEL, row),
        pl.BlockSpec((1, D_MODEL), const),
        pl.BlockSpec((D_MODEL, IN_COLS), const, pipeline_mode=pl.Buffered(1)),
        pl.BlockSpec((1, LANES), const),
        pl.BlockSpec((1, LANES), const),
    ]
    return pl.pallas_call(
        functools.partial(_proj_kernel, pad_blocks=pad_blocks, first_tail_step=first_tail_step),
        grid=(b, nt + pad_blocks),
        in_specs=in_specs, out_specs=out_specs, out_shape=out_shape,
        compiler_params=pltpu.CompilerParams(
            dimension_semantics=("arbitrary", "arbitrary"), vmem_limit_bytes=VMEM_LIMIT),
        name="project",
    )(x, nw, w_bf, qnw128, knw128)


def _sb_kernel(*refs, blk, n_new_older, n_cache):
    if n_cache:
        q_ref, kn_ref, vn_ref, kc_ref, vc_ref, uo_ref, o_ref, qm_ref, acc_ref, carry_ref = refs
    else:
        q_ref, kn_ref, vn_ref, uo_ref, o_ref, qm_ref, acc_ref, carry_ref = refs
    i = pl.program_id(1)
    lane_lo = lax.broadcasted_iota(jnp.int32, (blk, LANES), 1) < HEAD_DIM
    uo = uo_ref[...]

    for hp in range(HEAD_PAIRS):
        q2 = q_ref[0, :, hp * LANES:(hp + 1) * LANES]
        zero = jnp.zeros_like(q2)
        qm_ref[2 * hp] = jnp.where(lane_lo, q2, zero)
        qm_ref[2 * hp + 1] = jnp.where(lane_lo, zero, q2)
    acc_ref[...] = jnp.zeros(acc_ref.shape, F32)
    carry_ref[...] = jnp.zeros(carry_ref.shape, F32)

    def visit(k_blk, v_blk, causal):
        for h in range(N_HEADS):
            sl = slice((h // 2) * LANES, (h // 2 + 1) * LANES)
            z = _dot_nt(qm_ref[h], k_blk[:, sl])
            soft = jnp.log(1.0 + jnp.exp(-jnp.abs(z)))
            log_beta = jnp.minimum(z, 0.0) - soft
            log_keep = log_beta - z
            if causal is not None:
                log_keep = jnp.where(causal, log_keep, 0.0)
            hi = log_keep.astype(BF16)
            lo = (log_keep - hi.astype(F32)).astype(BF16)
            sc = _dot(jnp.concatenate([hi, lo], axis=1), uo)
            carry = carry_ref[h]
            w = jnp.exp(log_beta + sc[:, :blk] + carry)
            if causal is not None:
                w = jnp.where(causal, w, 0.0)
            carry_ref[h] = carry + sc[:, blk:]
            acc_ref[h] += _dot(w.astype(BF16), v_blk[:, sl])

    def alive():
        return jnp.max(carry_ref[...]) > EXP_ZERO_BELOW

    row = lax.broadcasted_iota(jnp.int32, (blk, blk), 0)
    colm = lax.broadcasted_iota(jnp.int32, (blk, blk), 1)
    start = pl.multiple_of(i * blk, blk)
    visit(kn_ref[0, pl.ds(start, blk), :], vn_ref[0, pl.ds(start, blk), :], colm < row)

    def older_new(c):
        j, _ = c
        s = pl.multiple_of(j * blk, blk)
        visit(kn_ref[0, pl.ds(s, blk), :], vn_ref[0, pl.ds(s, blk), :], None)
        return j - 1, alive()

    def cond(c):
        j, live = c
        return jnp.logical_and(j >= 0, live)

    if n_new_older:
        lax.while_loop(cond, older_new, (i - 1, alive()))

    if n_cache:
        def older_cache(c):
            j, _ = c
            s = pl.multiple_of(j * blk, blk)
            visit(kc_ref[0, pl.ds(s, blk), :].astype(BF16),
                  vc_ref[0, pl.ds(s, blk), :].astype(BF16), None)
            return j - 1, alive()
        lax.while_loop(cond, older_cache, (jnp.int32(n_cache - 1), alive()))

    for hp in range(HEAD_PAIRS):
        o_ref[0, :, hp * LANES:(hp + 1) * LANES] = jnp.where(
            lane_lo, acc_ref[2 * hp], acc_ref[2 * hp + 1])


def _cumsum_matrix(blk):
    j = np.arange(blk)[:, None]
    s = np.arange(blk)[None, :]
    u = (j > s).astype(np.float32)
    half = np.concatenate([u, np.ones((blk, blk), np.float32)], axis=1)
    return jnp.asarray(np.concatenate([half, half], axis=0), dtype=BF16)


def _stick_breaking(q, k_new, v_new, k_cache=None, v_cache=None, *, blk):
    b, tq, _ = q.shape
    nq = tq // blk
    n_cache = 0 if k_cache is None else k_cache.shape[1] // blk
    full = lambda arr: pl.BlockSpec((1,) + arr.shape[1:], lambda bi, i: (bi, 0, 0))
    qblk = pl.BlockSpec((1, blk, WIDTH), lambda bi, i: (bi, i, 0))
    uo = _cumsum_matrix(blk)
    operands = [q, k_new, v_new]
    in_specs = [qblk, full(k_new), full(v_new)]
    if n_cache:
        operands += [k_cache, v_cache]
        in_specs += [full(k_cache), full(v_cache)]
    operands.append(uo)
    in_specs.append(pl.BlockSpec(uo.shape, lambda bi, i: (0, 0)))
    return pl.pallas_call(
        functools.partial(_sb_kernel, blk=blk, n_new_older=nq - 1, n_cache=n_cache),
        grid=(b, nq),
        in_specs=in_specs, out_specs=qblk,
        out_shape=jax.ShapeDtypeStruct((b, tq, WIDTH), F32),
        scratch_shapes=[pltpu.VMEM((N_HEADS, blk, LANES), BF16),
                        pltpu.VMEM((N_HEADS, blk, LANES), F32),
                        pltpu.VMEM((N_HEADS, blk, blk), F32)],
        compiler_params=pltpu.CompilerParams(
            dimension_semantics=("arbitrary", "arbitrary"), vmem_limit_bytes=VMEM_LIMIT),
        name="stick_breaking",
    )(*operands)


def _bias_kernel(rb_ref, o_ref, *, g):
    rows, win = g * CHUNK, (LEFT_CHUNKS + g) * CHUNK
    n = 768
    npad = rb_ref.shape[1]
    jj = lax.broadcasted_iota(jnp.int32, (npad, n), 1)
    rr = lax.broadcasted_iota(jnp.int32, (npad, n), 0)
    idx = jnp.where(jj > win, 2 * REL_CLIP,
                    jnp.clip(BAND - jj, -REL_CLIP, REL_CLIP) + REL_CLIP)
    sel = (rr == idx).astype(BF16)
    rb = rb_ref[...]
    hi = rb.astype(BF16)
    r1 = rb - hi.astype(F32)
    mid = r1.astype(BF16)
    lo = (r1 - mid.astype(F32)).astype(BF16)
    line = _dot(hi, sel) + _dot(mid, sel) + _dot(lo, sel)
    tq = lax.broadcasted_iota(jnp.int32, (rows, win), 0)
    tk = lax.broadcasted_iota(jnp.int32, (rows, win), 1)
    off = tk - (tq // CHUNK) * CHUNK
    valid = jnp.logical_and(off >= 0, off < (LEFT_CHUNKS + 1) * CHUNK)
    for h in range(N_HEADS):
        tiled = jnp.broadcast_to(line[h:h + 1, :], (rows, n))
        rolled = pltpu.roll(tiled, 0, 1, stride=1, stride_axis=0)
        o_ref[h] = jnp.where(valid, rolled[:, :win], NEG_INF)


def _bias_table(rb_pad, g):
    rows, win = g * CHUNK, (LEFT_CHUNKS + g) * CHUNK
    return pl.pallas_call(
        functools.partial(_bias_kernel, g=g),
        out_shape=jax.ShapeDtypeStruct((N_HEADS, rows, win), F32),
        name="bias_table",
    )(rb_pad)


def _cb_kernel(q_ref, k_ref, v_ref, bias_ref, o_ref, *, rows, win, pad_history):
    p = pl.program_id(1)
    start = pl.multiple_of(p * rows, rows)
    lane_lo = lax.broadcasted_iota(jnp.int32, (rows, LANES), 1) < HEAD_DIM
    if pad_history:
        key_ok = lax.broadcasted_iota(jnp.int32, (rows, win), 1) >= BAND - p * rows
    for hp in range(HEAD_PAIRS):
        sl = slice(hp * LANES, (hp + 1) * LANES)
        q2 = q_ref[0, :, sl]
        k2 = k_ref[0, pl.ds(start, win), sl]
        v2 = v_ref[0, pl.ds(start, win), sl]
        zero = jnp.zeros_like(q2)
        outs = []
        for hh in range(2):
            qm = jnp.where(lane_lo, q2, zero) if hh == 0 else jnp.where(lane_lo, zero, q2)
            s = _dot_nt(qm, k2) + bias_ref[2 * hp + hh]
            if pad_history:
                s = jnp.where(key_ok, s, NEG_INF)
            m = jnp.max(s, axis=-1, keepdims=True)
            e = jnp.exp(s - m)
            l = jnp.sum(e, axis=-1, keepdims=True)
            outs.append(_dot(e.astype(BF16), v2) / l)
        o_ref[0, :, sl] = jnp.where(lane_lo, outs[0], outs[1])


def _chunk_band(q, k_hist, v_hist, bias, *, g, pad_history):
    b, t, _ = q.shape
    rows, win = g * CHUNK, (LEFT_CHUNKS + g) * CHUNK
    full = lambda arr: pl.BlockSpec((1,) + arr.shape[1:], lambda bi, i: (bi, 0, 0))
    qblk = pl.BlockSpec((1, rows, WIDTH), lambda bi, i: (bi, i, 0))
    return pl.pallas_call(
        functools.partial(_cb_kernel, rows=rows, win=win, pad_history=pad_history),
        grid=(b, t // rows),
        in_specs=[qblk, full(k_hist), full(v_hist),
                  pl.BlockSpec(bias.shape, lambda bi, i: (0, 0, 0))],
        out_specs=qblk,
        out_shape=jax.ShapeDtypeStruct((b, t, WIDTH), F32),
        compiler_params=pltpu.CompilerParams(
            dimension_semantics=("arbitrary", "arbitrary"), vmem_limit_bytes=VMEM_LIMIT),
        name="chunk_band",
    )(q, k_hist, v_hist, bias)


def _merge_kernel(x_ref, osb_ref, zsb_ref, ocb_ref, zcb_ref, gsb_ref, gcb_ref,
                  wsb_ref, wcb_ref, wout_ref, y_ref):
    zsb = zsb_ref[0]
    zcb = zcb_ref[0]
    a_sb = (osb_ref[0] * (zsb * _sigmoid(zsb))).astype(BF16)
    a_cb = (ocb_ref[0] * (zcb * _sigmoid(zcb))).astype(BF16)
    b_sb = _dot(a_sb, wsb_ref[...])
    b_cb = _dot(a_cb, wcb_ref[...])
    h = _sigmoid(gsb_ref[0]) * b_sb + _sigmoid(gcb_ref[0]) * b_cb
    y_ref[0] = x_ref[0] + _dot(h.astype(BF16), wout_ref[...])


def _merge(x, osb, zsb, ocb, zcb, gsb, gcb, wsb, wcb, wout, *, tm):
    b, t, _ = x.shape
    wide = pl.BlockSpec((1, tm, D_MODEL), lambda bi, i: (bi, i, 0))
    narrow = pl.BlockSpec((1, tm, WIDTH), lambda bi, i: (bi, i, 0))
    const = lambda arr: pl.BlockSpec(arr.shape, lambda bi, i: (0, 0))
    return pl.pallas_call(
        _merge_kernel,
        grid=(b, t // tm),
        in_specs=[wide, narrow, narrow, narrow, narrow, wide, wide,
                  const(wsb), const(wcb), const(wout)],
        out_specs=wide,
        out_shape=jax.ShapeDtypeStruct((b, t, D_MODEL), F32),
        compiler_params=pltpu.CompilerParams(
            dimension_semantics=("arbitrary", "arbitrary"), vmem_limit_bytes=VMEM_LIMIT),
        name="merge",
    )(x, osb, zsb, ocb, zcb, gsb, gcb, wsb, wcb, wout)


def _heads(a):
    b, t, _ = a.shape
    return a.reshape(b, t, N_HEADS, HEAD_DIM)


def kernel(x_prompt, x_sample, cache_sb_k, cache_sb_v, cache_cb_k, cache_cb_v, norm_w, w_in,
           q_norm_w, k_norm_w, rel_bias, w_proj_sb, w_proj_cb, w_out):
    depth = w_in.shape[0]
    b, t, _ = x_prompt.shape
    bs, tn, _ = x_sample.shape
    p = cache_sb_k.shape[2]
    r = cache_cb_k.shape[2]
    assert t % (2 * CHUNK) == 0 and tn == CHUNK and r == BAND and p % tn == 0

    y_p, y_s = x_prompt, x_sample
    outs = [[] for _ in range(8)]
    for l in range(depth):
        nw = norm_w[l].reshape(1, D_MODEL)
        w_bf = w_in[l].astype(BF16)
        qnw = jnp.tile(q_norm_w[l], 2).reshape(1, LANES)
        knw = jnp.tile(k_norm_w[l], 2).reshape(1, LANES)
        rb_pad = jnp.pad(rel_bias[l], ((0, 0), (0, 384 - (2 * REL_CLIP + 1))))
        bias2 = _bias_table(rb_pad, 2)
        bias1 = bias2[:, :CHUNK, :(LEFT_CHUNKS + 1) * CHUNK]
        wsb = w_proj_sb[l].astype(BF16)
        wcb = w_proj_cb[l].astype(BF16)
        wout = w_out[l].astype(BF16)

        (sbq, sbk32, sbk16, sbv32, sbv16, sbz, cbq, cbk16, cbv16, cbk32, cbv32, cbz,
         gsb, gcb) = _project(y_p, nw, w_bf, qnw, knw, tm=256, pad_rows=BAND)
        o_sb = _stick_breaking(sbq, sbk16, sbv16, blk=128)
        o_cb = _chunk_band(cbq, cbk16, cbv16, bias2, g=2, pad_history=True)
        y_p_next = _merge(y_p, o_sb, sbz, o_cb, cbz, gsb, gcb, wsb, wcb, wout, tm=256)
        for lst, a in zip(outs[:4], (sbk32, sbv32, cbk32, cbv32)):
            lst.append(_heads(a))

        (sbq, sbk32, sbk16, sbv32, sbv16, sbz, cbq, cbk16, cbv16, cbk32, cbv32, cbz,
         gsb, gcb) = _project(y_s, nw, w_bf, qnw, knw, tm=tn, pad_rows=0)
        o_sb = _stick_breaking(sbq, sbk16, sbv16,
                               cache_sb_k[l].reshape(bs, p, WIDTH),
                               cache_sb_v[l].reshape(bs, p, WIDTH), blk=tn)
        k_hist = jnp.concatenate([cache_cb_k[l].reshape(bs, r, WIDTH).astype(BF16), cbk16], axis=1)
        v_hist = jnp.concatenate([cache_cb_v[l].reshape(bs, r, WIDTH).astype(BF16), cbv16], axis=1)
        o_cb = _chunk_band(cbq, k_hist, v_hist, bias1, g=1, pad_history=False)
        y_s_next = _merge(y_s, o_sb, sbz, o_cb, cbz, gsb, gcb, wsb, wcb, wout, tm=tn)
        for lst, a in zip(outs[4:], (sbk32, sbv32, cbk32, cbv32)):
            lst.append(_heads(a))

        y_p, y_s = y_p_next, y_s_next

    return (y_p, y_s) + tuple(jnp.stack(o) for o in outs)
```

```python
import functools

import jax
import jax.numpy as jnp
import numpy as np
from jax import lax
from jax.experimental import pallas as pl
from jax.experimental.pallas import tpu as pltpu

F32 = jnp.float32
BF16 = jnp.bfloat16

D_MODEL = 1024
CHUNK = 64
LEFT_CHUNKS = 8
BAND = LEFT_CHUNKS * CHUNK
N_HEADS = 8
HEAD_DIM = 64
WIDTH = N_HEADS * HEAD_DIM
REL_CLIP = 128
EPS = 1e-6
NEG_INF = -1e30
ATTN_SCALE = 0.125
IN_COLS = 4 * WIDTH + 4 * WIDTH + 2 * D_MODEL

LANES = 128
HEAD_PAIRS = N_HEADS // 2
EXP_ZERO_BELOW = -104.0
VMEM_LIMIT = 56 * 1024 * 1024


def _dot(a, b):
    return jnp.dot(a, b, preferred_element_type=F32)


def _dot_nt(a, b):
    return lax.dot_general(a, b, (((1,), (1,)), ((), ())), preferred_element_type=F32)


def _sigmoid(x):
    return 1.0 / (1.0 + jnp.exp(-x))


def _head_rms_norm(h, w128, lane_lo):
    sq = h * h
    s_lo = jnp.sum(jnp.where(lane_lo, sq, 0.0), axis=-1, keepdims=True)
    s_hi = jnp.sum(jnp.where(lane_lo, 0.0, sq), axis=-1, keepdims=True)
    r_lo = lax.rsqrt(s_lo * (1.0 / HEAD_DIM) + EPS)
    r_hi = lax.rsqrt(s_hi * (1.0 / HEAD_DIM) + EPS)
    return (h * jnp.where(lane_lo, r_lo, r_hi)) * w128


def _proj_kernel(x_ref, nw_ref, w_ref, qnw_ref, knw_ref,
                 sbq_ref, sbk32_ref, sbk16_ref, sbv32_ref, sbv16_ref, sbz_ref,
                 cbq_ref, cbk16_ref, cbv16_ref, cbk32_ref, cbv32_ref, cbz_ref,
                 gsb_ref, gcb_ref, *, pad_blocks, first_tail_step):
    j = pl.program_id(1)

    if pad_blocks:
        @pl.when(j < pad_blocks)
        def _():
            cbk16_ref[...] = jnp.zeros(cbk16_ref.shape, BF16)
            cbv16_ref[...] = jnp.zeros(cbv16_ref.shape, BF16)

    @pl.when(j >= pad_blocks)
    def _():
        x = x_ref[0]
        ms = jnp.mean(x * x, axis=-1, keepdims=True)
        xb = ((x * lax.rsqrt(ms + EPS)) * nw_ref[...]).astype(BF16)
        tm = x.shape[0]
        lane_lo = lax.broadcasted_iota(jnp.int32, (tm, LANES), 1) < HEAD_DIM

        def col(c0, width):
            return _dot(xb, w_ref[:, c0:c0 + width])

        sbq_ref[0] = (col(0, WIDTH) * ATTN_SCALE).astype(BF16)
        h = col(WIDTH, WIDTH)
        sbk32_ref[0] = h
        sbk16_ref[0] = h.astype(BF16)
        h = col(2 * WIDTH, WIDTH)
        sbv32_ref[0] = h
        sbv16_ref[0] = h.astype(BF16)
        sbz_ref[0] = col(3 * WIDTH, WIDTH)

        for s in range(HEAD_PAIRS):
            sl = slice(s * LANES, (s + 1) * LANES)
            hq = _head_rms_norm(col(4 * WIDTH + s * LANES, LANES), qnw_ref[...], lane_lo)
            cbq_ref[0, :, sl] = (hq * ATTN_SCALE).astype(BF16)
            hk = _head_rms_norm(col(5 * WIDTH + s * LANES, LANES), knw_ref[...], lane_lo)
            cbk16_ref[0, :, sl] = hk.astype(BF16)

            @pl.when(j >= first_tail_step)
            def _():
                cbk32_ref[0, :, sl] = hk

        h = col(6 * WIDTH, WIDTH)
        cbv16_ref[0] = h.astype(BF16)

        @pl.when(j >= first_tail_step)
        def _():
            cbv32_ref[0] = h

        cbz_ref[0] = col(7 * WIDTH, WIDTH)
        gsb_ref[0] = col(8 * WIDTH, D_MODEL)
        gcb_ref[0] = col(8 * WIDTH + D_MODEL, D_MODEL)


def _project(x, nw, w_bf, qnw128, knw128, *, tm, pad_rows):
    b, t, _ = x.shape
    nt = t // tm
    pad_blocks = pad_rows // tm
    tail = min(BAND, t)
    tail_blocks = tail // tm
    first_tail_step = pad_blocks + nt - tail_blocks

    def row(bi, j):
        return (bi, jnp.maximum(j - pad_blocks, 0), 0)

    def padded(bi, j):
        return (bi, j, 0)

    def tail_map(bi, j):
        return (bi, jnp.maximum(j - first_tail_step, 0), 0)

    def const(bi, j):
        return (0, 0)

    def out(width, dtype, rows=t):
        return jax.ShapeDtypeStruct((b, rows, width), dtype)

    blk = lambda width, imap: pl.BlockSpec((1, tm, width), imap)
    out_shape = [
        out(WIDTH, BF16), out(WIDTH, F32), out(WIDTH, BF16), out(WIDTH, F32), out(WIDTH, BF16),
        out(WIDTH, F32),
        out(WIDTH, BF16), out(WIDTH, BF16, t + pad_rows), out(WIDTH, BF16, t + pad_rows),
        out(WIDTH, F32, tail), out(WIDTH, F32, tail), out(WIDTH, F32),
        out(D_MODEL, F32), out(D_MODEL, F32),
    ]
    out_specs = [
        blk(WIDTH, row), blk(WIDTH, row), blk(WIDTH, row), blk(WIDTH, row), blk(WIDTH, row),
        blk(WIDTH, row),
        blk(WIDTH, row), blk(WIDTH, padded), blk(WIDTH, padded),
        blk(WIDTH, tail_map), blk(WIDTH, tail_map), blk(WIDTH, row),
        blk(D_MODEL, row), blk(D_MODEL, row),
    ]
    in_specs = [
        blk(D_MODEL, row),
        pl.BlockSpec((1, D_MODEL), const),
        pl.BlockSpec((D_MODEL, IN_COLS), const, pipeline_mode=pl.Buffered(1)),
        pl.BlockSpec((1, LANES), const),
        pl.BlockSpec((1, LANES), const),
    ]
    return pl.pallas_call(
        functools.partial(_proj_kernel, pad_blocks=pad_blocks, first_tail_step=first_tail_step),
        grid=(b, nt + pad_blocks),
        in_specs=in_specs, out_specs=out_specs, out_shape=out_shape,
        compiler_params=pltpu.CompilerParams(
            dimension_semantics=("arbitrary", "arbitrary"), vmem_limit_bytes=VMEM_LIMIT),
        name="project",
    )(x, nw, w_bf, qnw128, knw128)


def _sb_kernel(*refs, blk, n_new_older, n_cache):
    if n_cache:
        (q_ref, kn_ref, vn_ref, kc_ref, vc_ref, uo_ref, o_ref,
         qm_ref, acc_ref, carry_ref, z_ref, lb_ref, sc_ref) = refs
    else:
        (q_ref, kn_ref, vn_ref, uo_ref, o_ref,
         qm_ref, acc_ref, carry_ref, z_ref, lb_ref, sc_ref) = refs
    i = pl.program_id(1)
    lane_lo = lax.broadcasted_iota(jnp.int32, (blk, LANES), 1) < HEAD_DIM
    uo = uo_ref[...]

    for hp in range(HEAD_PAIRS):
        q2 = q_ref[0, :, hp * LANES:(hp + 1) * LANES]
        zero = jnp.zeros_like(q2)
        qm_ref[2 * hp] = jnp.where(lane_lo, q2, zero)
        qm_ref[2 * hp + 1] = jnp.where(lane_lo, zero, q2)
    acc_ref[...] = jnp.zeros(acc_ref.shape, F32)
    carry_ref[...] = jnp.zeros(carry_ref.shape, F32)

    def visit(k_blk, v_blk, causal):
        col = lambda h: slice((h // 2) * LANES, (h // 2 + 1) * LANES)
        for h in range(N_HEADS):
            z_ref[h] = _dot_nt(qm_ref[h], k_blk[:, col(h)])
        for h in range(N_HEADS):
            zh = z_ref[h]
            soft = jnp.log(1.0 + jnp.exp(-jnp.abs(zh)))
            log_beta = jnp.minimum(zh, 0.0) - soft
            log_keep = log_beta - zh
            if causal is not None:
                log_keep = jnp.where(causal, log_keep, 0.0)
            lb_ref[h] = log_beta
            hi = log_keep.astype(BF16)
            lo = (log_keep - hi.astype(F32)).astype(BF16)
            sc_ref[h] = _dot(jnp.concatenate([hi, lo], axis=1), uo)
        for h in range(N_HEADS):
            carry = carry_ref[h]
            w = jnp.exp(lb_ref[h] + sc_ref[h, :, :blk] + carry)
            if causal is not None:
                w = jnp.where(causal, w, 0.0)
            carry_ref[h] = carry + sc_ref[h, :, blk:]
            acc_ref[h] += _dot(w.astype(BF16), v_blk[:, col(h)])

    def alive():
        return jnp.max(carry_ref[...]) > EXP_ZERO_BELOW

    row = lax.broadcasted_iota(jnp.int32, (blk, blk), 0)
    colm = lax.broadcasted_iota(jnp.int32, (blk, blk), 1)
    start = pl.multiple_of(i * blk, blk)
    visit(kn_ref[0, pl.ds(start, blk), :], vn_ref[0, pl.ds(start, blk), :], colm < row)

    def older_new(c):
        j, _ = c
        s = pl.multiple_of(j * blk, blk)
        visit(kn_ref[0, pl.ds(s, blk), :], vn_ref[0, pl.ds(s, blk), :], None)
        return j - 1, alive()

    def cond(c):
        j, live = c
        return jnp.logical_and(j >= 0, live)

    if n_new_older:
        lax.while_loop(cond, older_new, (i - 1, alive()))

    if n_cache:
        def older_cache(c):
            j, _ = c
            s = pl.multiple_of(j * blk, blk)
            visit(kc_ref[0, pl.ds(s, blk), :].astype(BF16),
                  vc_ref[0, pl.ds(s, blk), :].astype(BF16), None)
            return j - 1, alive()
        lax.while_loop(cond, older_cache, (jnp.int32(n_cache - 1), alive()))

    for hp in range(HEAD_PAIRS):
        o_ref[0, :, hp * LANES:(hp + 1) * LANES] = jnp.where(
            lane_lo, acc_ref[2 * hp], acc_ref[2 * hp + 1])


def _cumsum_matrix(blk):
    j = np.arange(blk)[:, None]
    s = np.arange(blk)[None, :]
    u = (j > s).astype(np.float32)
    half = np.concatenate([u, np.ones((blk, blk), np.float32)], axis=1)
    return jnp.asarray(np.concatenate([half, half], axis=0), dtype=BF16)


def _stick_breaking(q, k_new, v_new, k_cache=None, v_cache=None, *, blk):
    b, tq, _ = q.shape
    nq = tq // blk
    n_cache = 0 if k_cache is None else k_cache.shape[1] // blk
    full = lambda arr: pl.BlockSpec((1,) + arr.shape[1:], lambda bi, i: (bi, 0, 0))
    qblk = pl.BlockSpec((1, blk, WIDTH), lambda bi, i: (bi, i, 0))
    uo = _cumsum_matrix(blk)
    operands = [q, k_new, v_new]
    in_specs = [qblk, full(k_new), full(v_new)]
    if n_cache:
        operands += [k_cache, v_cache]
        in_specs += [full(k_cache), full(v_cache)]
    operands.append(uo)
    in_specs.append(pl.BlockSpec(uo.shape, lambda bi, i: (0, 0)))
    return pl.pallas_call(
        functools.partial(_sb_kernel, blk=blk, n_new_older=nq - 1, n_cache=n_cache),
        grid=(b, nq),
        in_specs=in_specs, out_specs=qblk,
        out_shape=jax.ShapeDtypeStruct((b, tq, WIDTH), F32),
        scratch_shapes=[pltpu.VMEM((N_HEADS, blk, LANES), BF16),
                        pltpu.VMEM((N_HEADS, blk, LANES), F32),
                        pltpu.VMEM((N_HEADS, blk, blk), F32),
                        pltpu.VMEM((N_HEADS, blk, blk), F32),
                        pltpu.VMEM((N_HEADS, blk, blk), F32),
                        pltpu.VMEM((N_HEADS, blk, 2 * blk), F32)],
        compiler_params=pltpu.CompilerParams(
            dimension_semantics=("arbitrary", "arbitrary"), vmem_limit_bytes=VMEM_LIMIT),
        name="stick_breaking",
    )(*operands)


def _bias_kernel(rb_ref, o_ref, *, g):
    rows, win = g * CHUNK, (LEFT_CHUNKS + g) * CHUNK
    n = 768
    npad = rb_ref.shape[1]
    jj = lax.broadcasted_iota(jnp.int32, (npad, n), 1)
    rr = lax.broadcasted_iota(jnp.int32, (npad, n), 0)
    idx = jnp.where(jj > win, 2 * REL_CLIP,
                    jnp.clip(BAND - jj, -REL_CLIP, REL_CLIP) + REL_CLIP)
    sel = (rr == idx).astype(BF16)
    rb = rb_ref[...]
    hi = rb.astype(BF16)
    r1 = rb - hi.astype(F32)
    mid = r1.astype(BF16)
    lo = (r1 - mid.astype(F32)).astype(BF16)
    line = _dot(hi, sel) + _dot(mid, sel) + _dot(lo, sel)
    tq = lax.broadcasted_iota(jnp.int32, (rows, win), 0)
    tk = lax.broadcasted_iota(jnp.int32, (rows, win), 1)
    off = tk - (tq // CHUNK) * CHUNK
    valid = jnp.logical_and(off >= 0, off < (LEFT_CHUNKS + 1) * CHUNK)
    for h in range(N_HEADS):
        tiled = jnp.broadcast_to(line[h:h + 1, :], (rows, n))
        rolled = pltpu.roll(tiled, 0, 1, stride=1, stride_axis=0)
        o_ref[h] = jnp.where(valid, rolled[:, :win], NEG_INF)


def _bias_table(rb_pad, g):
    rows, win = g * CHUNK, (LEFT_CHUNKS + g) * CHUNK
    return pl.pallas_call(
        functools.partial(_bias_kernel, g=g),
        out_shape=jax.ShapeDtypeStruct((N_HEADS, rows, win), F32),
        name="bias_table",
    )(rb_pad)


def _cb_kernel(q_ref, k_ref, v_ref, bias_ref, o_ref, s_ref, *, rows, win, pad_history):
    p = pl.program_id(1)
    start = pl.multiple_of(p * rows, rows)
    lane_lo = lax.broadcasted_iota(jnp.int32, (rows, LANES), 1) < HEAD_DIM
    if pad_history:
        key_ok = lax.broadcasted_iota(jnp.int32, (rows, win), 1) >= BAND - p * rows
    col = lambda h: slice((h // 2) * LANES, (h // 2 + 1) * LANES)

    def scores(h):
        q2 = q_ref[0, :, col(h)]
        zero = jnp.zeros_like(q2)
        qm = jnp.where(lane_lo, q2, zero) if h % 2 == 0 else jnp.where(lane_lo, zero, q2)
        return _dot_nt(qm, k_ref[0, pl.ds(start, win), col(h)])

    for h in range(N_HEADS):
        s = scores(h) + bias_ref[h]
        if pad_history:
            s = jnp.where(key_ok, s, NEG_INF)
        s_ref[h] = s
    outs = {}
    for h in range(N_HEADS):
        s = s_ref[h]
        e = jnp.exp(s - jnp.max(s, axis=-1, keepdims=True))
        l = jnp.sum(e, axis=-1, keepdims=True)
        outs[h] = _dot(e.astype(BF16), v_ref[0, pl.ds(start, win), col(h)]) / l
        if h % 2:
            o_ref[0, :, col(h)] = jnp.where(lane_lo, outs.pop(h - 1), outs.pop(h))


def _chunk_band(q, k_hist, v_hist, bias, *, g, pad_history):
    b, t, _ = q.shape
    rows, win = g * CHUNK, (LEFT_CHUNKS + g) * CHUNK
    full = lambda arr: pl.BlockSpec((1,) + arr.shape[1:], lambda bi, i: (bi, 0, 0))
    qblk = pl.BlockSpec((1, rows, WIDTH), lambda bi, i: (bi, i, 0))
    return pl.pallas_call(
        functools.partial(_cb_kernel, rows=rows, win=win, pad_history=pad_history),
        grid=(b, t // rows),
        in_specs=[qblk, full(k_hist), full(v_hist),
                  pl.BlockSpec(bias.shape, lambda bi, i: (0, 0, 0))],
        out_specs=qblk,
        out_shape=jax.ShapeDtypeStruct((b, t, WIDTH), F32),
        scratch_shapes=[pltpu.VMEM((N_HEADS, rows, win), F32)],
        compiler_params=pltpu.CompilerParams(
            dimension_semantics=("arbitrary", "arbitrary"), vmem_limit_bytes=VMEM_LIMIT),
        name="chunk_band",
    )(q, k_hist, v_hist, bias)


def _merge_kernel(x_ref, osb_ref, zsb_ref, ocb_ref, zcb_ref, gsb_ref, gcb_ref,
                  wsb_ref, wcb_ref, wout_ref, y_ref):
    zsb = zsb_ref[0]
    zcb = zcb_ref[0]
    a_sb = (osb_ref[0] * (zsb * _sigmoid(zsb))).astype(BF16)
    a_cb = (ocb_ref[0] * (zcb * _sigmoid(zcb))).astype(BF16)
    b_sb = _dot(a_sb, wsb_ref[...])
    b_cb = _dot(a_cb, wcb_ref[...])
    h = _sigmoid(gsb_ref[0]) * b_sb + _sigmoid(gcb_ref[0]) * b_cb
    y_ref[0] = x_ref[0] + _dot(h.astype(BF16), wout_ref[...])


def _merge(x, osb, zsb, ocb, zcb, gsb, gcb, wsb, wcb, wout, *, tm):
    b, t, _ = x.shape
    wide = pl.BlockSpec((1, tm, D_MODEL), lambda bi, i: (bi, i, 0))
    narrow = pl.BlockSpec((1, tm, WIDTH), lambda bi, i: (bi, i, 0))
    const = lambda arr: pl.BlockSpec(arr.shape, lambda bi, i: (0, 0))
    return pl.pallas_call(
        _merge_kernel,
        grid=(b, t // tm),
        in_specs=[wide, narrow, narrow, narrow, narrow, wide, wide,
                  const(wsb), const(wcb), const(wout)],
        out_specs=wide,
        out_shape=jax.ShapeDtypeStruct((b, t, D_MODEL), F32),
        compiler_params=pltpu.CompilerParams(
            dimension_semantics=("arbitrary", "arbitrary"), vmem_limit_bytes=VMEM_LIMIT),
        name="merge",
    )(x, osb, zsb, ocb, zcb, gsb, gcb, wsb, wcb, wout)


def _heads(a):
    b, t, _ = a.shape
    return a.reshape(b, t, N_HEADS, HEAD_DIM)


def kernel(x_prompt, x_sample, cache_sb_k, cache_sb_v, cache_cb_k, cache_cb_v, norm_w, w_in,
           q_norm_w, k_norm_w, rel_bias, w_proj_sb, w_proj_cb, w_out):
    depth = w_in.shape[0]
    b, t, _ = x_prompt.shape
    bs, tn, _ = x_sample.shape
    p = cache_sb_k.shape[2]
    r = cache_cb_k.shape[2]
    assert t % (2 * CHUNK) == 0 and tn == CHUNK and r == BAND and p % tn == 0

    y_p, y_s = x_prompt, x_sample
    outs = [[] for _ in range(8)]
    for l in range(depth):
        nw = norm_w[l].reshape(1, D_MODEL)
        w_bf = w_in[l].astype(BF16)
        qnw = jnp.tile(q_norm_w[l], 2).reshape(1, LANES)
        knw = jnp.tile(k_norm_w[l], 2).reshape(1, LANES)
        rb_pad = jnp.pad(rel_bias[l], ((0, 0), (0, 384 - (2 * REL_CLIP + 1))))
        bias2 = _bias_table(rb_pad, 2)
        bias1 = bias2[:, :CHUNK, :(LEFT_CHUNKS + 1) * CHUNK]
        wsb = w_proj_sb[l].astype(BF16)
        wcb = w_proj_cb[l].astype(BF16)
        wout = w_out[l].astype(BF16)

        (sbq, sbk32, sbk16, sbv32, sbv16, sbz, cbq, cbk16, cbv16, cbk32, cbv32, cbz,
         gsb, gcb) = _project(y_p, nw, w_bf, qnw, knw, tm=256, pad_rows=BAND)
        o_sb = _stick_breaking(sbq, sbk16, sbv16, blk=128)
        o_cb = _chunk_band(cbq, cbk16, cbv16, bias2, g=2, pad_history=True)
        y_p_next = _merge(y_p, o_sb, sbz, o_cb, cbz, gsb, gcb, wsb, wcb, wout, tm=256)
        for lst, a in zip(outs[:4], (sbk32, sbv32, cbk32, cbv32)):
            lst.append(_heads(a))

        (sbq, sbk32, sbk16, sbv32, sbv16, sbz, cbq, cbk16, cbv16, cbk32, cbv32, cbz,
         gsb, gcb) = _project(y_s, nw, w_bf, qnw, knw, tm=tn, pad_rows=0)
        o_sb = _stick_breaking(sbq, sbk16, sbv16,
                               cache_sb_k[l].reshape(bs, p, WIDTH),
                               cache_sb_v[l].reshape(bs, p, WIDTH), blk=tn)
        k_hist = jnp.concatenate([cache_cb_k[l].reshape(bs, r, WIDTH).astype(BF16), cbk16], axis=1)
        v_hist = jnp.concatenate([cache_cb_v[l].reshape(bs, r, WIDTH).astype(BF16), cbv16], axis=1)
        o_cb = _chunk_band(cbq, k_hist, v_hist, bias1, g=1, pad_history=False)
        y_s_next = _merge(y_s, o_sb, sbz, o_cb, cbz, gsb, gcb, wsb, wcb, wout, tm=tn)
        for lst, a in zip(outs[4:], (sbk32, sbv32, cbk32, cbv32)):
            lst.append(_heads(a))

        y_p, y_s = y_p_next, y_s_next

    return (y_p, y_s) + tuple(jnp.stack(o) for o in outs)
```

```python
import functools

import jax
import jax.numpy as jnp
import numpy as np
from jax import lax
from jax.experimental import pallas as pl
from jax.experimental.pallas import tpu as pltpu

F32 = jnp.float32
BF16 = jnp.bfloat16

D_MODEL = 1024
CHUNK = 64
LEFT_CHUNKS = 8
BAND = LEFT_CHUNKS * CHUNK
N_HEADS = 8
HEAD_DIM = 64
WIDTH = N_HEADS * HEAD_DIM
REL_CLIP = 128
EPS = 1e-6
NEG_INF = -1e30
ATTN_SCALE = 0.125
IN_COLS = 4 * WIDTH + 4 * WIDTH + 2 * D_MODEL

LANES = 128
HEAD_PAIRS = N_HEADS // 2
EXP_ZERO_BELOW = -104.0
LOG2E = 1.4426950408889634
VMEM_LIMIT = 56 * 1024 * 1024


def _dot(a, b):
    return jnp.dot(a, b, preferred_element_type=F32)


def _dot_nt(a, b):
    return lax.dot_general(a, b, (((1,), (1,)), ((), ())), preferred_element_type=F32)


def _sigmoid(x):
    return 1.0 / (1.0 + jnp.exp(-x))


def _head_rms_norm(h, w128, lane_lo):
    sq = h * h
    s_lo = jnp.sum(jnp.where(lane_lo, sq, 0.0), axis=-1, keepdims=True)
    s_hi = jnp.sum(jnp.where(lane_lo, 0.0, sq), axis=-1, keepdims=True)
    r_lo = lax.rsqrt(s_lo * (1.0 / HEAD_DIM) + EPS)
    r_hi = lax.rsqrt(s_hi * (1.0 / HEAD_DIM) + EPS)
    return (h * jnp.where(lane_lo, r_lo, r_hi)) * w128


def _proj_kernel(x_ref, nw_ref, w_ref, qnw_ref, knw_ref,
                 sbq_ref, sbk32_ref, sbk16_ref, sbv32_ref, sbv16_ref, sbz_ref,
                 cbq_ref, cbk16_ref, cbv16_ref, cbk32_ref, cbv32_ref, cbz_ref,
                 gsb_ref, gcb_ref, *, pad_blocks):
    j = pl.program_id(1)

    if pad_blocks:
        @pl.when(j < pad_blocks)
        def _():
            for ref in (sbk16_ref, sbv16_ref, cbk16_ref, cbv16_ref):
                ref[...] = jnp.zeros(ref.shape, BF16)

    @pl.when(j >= pad_blocks)
    def _():
        x = x_ref[0]
        ms = jnp.mean(x * x, axis=-1, keepdims=True)
        xb = ((x * lax.rsqrt(ms + EPS)) * nw_ref[...]).astype(BF16)
        tm = x.shape[0]
        lane_lo = lax.broadcasted_iota(jnp.int32, (tm, LANES), 1) < HEAD_DIM

        def col(c0, width):
            return _dot(xb, w_ref[:, c0:c0 + width])

        sbq_ref[0] = (col(0, WIDTH) * ATTN_SCALE).astype(BF16)
        h = col(WIDTH, WIDTH)
        sbk32_ref[0] = h
        sbk16_ref[0] = h.astype(BF16)
        h = col(2 * WIDTH, WIDTH)
        sbv32_ref[0] = h
        sbv16_ref[0] = h.astype(BF16)
        sbz_ref[0] = col(3 * WIDTH, WIDTH)

        hq_all = col(4 * WIDTH, WIDTH)
        hk_all = col(5 * WIDTH, WIDTH)
        for s in range(HEAD_PAIRS):
            sl = slice(s * LANES, (s + 1) * LANES)
            hq = _head_rms_norm(hq_all[:, sl], qnw_ref[...], lane_lo)
            cbq_ref[0, :, sl] = (hq * ATTN_SCALE).astype(BF16)
            hk = _head_rms_norm(hk_all[:, sl], knw_ref[...], lane_lo)
            cbk16_ref[0, :, sl] = hk.astype(BF16)
            cbk32_ref[0, :, sl] = hk

        h = col(6 * WIDTH, WIDTH)
        cbv16_ref[0] = h.astype(BF16)
        cbv32_ref[0] = h

        cbz_ref[0] = col(7 * WIDTH, WIDTH)
        gsb_ref[0] = col(8 * WIDTH, D_MODEL)
        gcb_ref[0] = col(8 * WIDTH + D_MODEL, D_MODEL)


def _project(x, nw, w_bf, qnw128, knw128, *, tm, pad_rows):
    b, t, _ = x.shape
    nt = t // tm
    pad_blocks = pad_rows // tm
    tail = min(BAND, t)
    tail_blocks = tail // tm
    first_tail_step = pad_blocks + nt - tail_blocks

    def row(bi, j):
        return (bi, jnp.maximum(j - pad_blocks, 0), 0)

    def padded(bi, j):
        return (bi, j, 0)

    def tail_map(bi, j):
        return (bi, jnp.maximum(j - first_tail_step, 0), 0)

    def const(bi, j):
        return (0, 0)

    def out(width, dtype, rows=t):
        return jax.ShapeDtypeStruct((b, rows, width), dtype)

    blk = lambda width, imap: pl.BlockSpec((1, tm, width), imap)
    out_shape = [
        out(WIDTH, BF16), out(WIDTH, F32), out(WIDTH, BF16, t + pad_rows),
        out(WIDTH, F32), out(WIDTH, BF16, t + pad_rows), out(WIDTH, F32),
        out(WIDTH, BF16), out(WIDTH, BF16, t + pad_rows), out(WIDTH, BF16, t + pad_rows),
        out(WIDTH, F32, tail), out(WIDTH, F32, tail), out(WIDTH, F32),
        out(D_MODEL, F32), out(D_MODEL, F32),
    ]
    out_specs = [
        blk(WIDTH, row), blk(WIDTH, row), blk(WIDTH, padded),
        blk(WIDTH, row), blk(WIDTH, padded), blk(WIDTH, row),
        blk(WIDTH, row), blk(WIDTH, padded), blk(WIDTH, padded),
        blk(WIDTH, tail_map), blk(WIDTH, tail_map), blk(WIDTH, row),
        blk(D_MODEL, row), blk(D_MODEL, row),
    ]
    in_specs = [
        blk(D_MODEL, row),
        pl.BlockSpec((1, D_MODEL), const),
        pl.BlockSpec((D_MODEL, IN_COLS), const, pipeline_mode=pl.Buffered(1)),
        pl.BlockSpec((1, LANES), const),
        pl.BlockSpec((1, LANES), const),
    ]
    return pl.pallas_call(
        functools.partial(_proj_kernel, pad_blocks=pad_blocks),
        grid=(b, nt + pad_blocks),
        in_specs=in_specs, out_specs=out_specs, out_shape=out_shape,
        compiler_params=pltpu.CompilerParams(
            dimension_semantics=("arbitrary", "arbitrary"), vmem_limit_bytes=VMEM_LIMIT),
        name="project",
    )(x, nw, w_bf, qnw128, knw128)


def _sb_kernel(*refs, blk, new_pad, n_cache):
    if n_cache:
        (q_ref, kn_ref, vn_ref, kc_ref, vc_ref, uo_ref, o_ref,
         qm_ref, acc_ref, carry_ref, z_ref, sc_ref) = refs
    else:
        (q_ref, kn_ref, vn_ref, uo_ref, o_ref,
         qm_ref, acc_ref, carry_ref, z_ref, sc_ref) = refs
    i = pl.program_id(1)
    lane_lo = lax.broadcasted_iota(jnp.int32, (blk, LANES), 1) < HEAD_DIM
    uo = uo_ref[...]

    for hp in range(HEAD_PAIRS):
        q2 = q_ref[0, :, hp * LANES:(hp + 1) * LANES]
        zero = jnp.zeros_like(q2)
        qm_ref[2 * hp] = jnp.where(lane_lo, q2, zero)
        qm_ref[2 * hp + 1] = jnp.where(lane_lo, zero, q2)
    acc_ref[...] = jnp.zeros(acc_ref.shape, F32)
    carry_ref[...] = jnp.zeros(carry_ref.shape, F32)

    def visit(k_blk, v_blk, causal):
        n = k_blk.shape[0] // blk
        col = lambda h: slice((h // 2) * LANES, (h // 2 + 1) * LANES)
        sub = lambda u: slice(u * blk, (u + 1) * blk)
        for h in range(N_HEADS):
            z_ref[h, :, :n * blk] = _dot_nt(qm_ref[h], k_blk[:, col(h)])
        for h in range(N_HEADS):
            zh = z_ref[h, :, :n * blk]
            sp = jnp.maximum(zh, 0.0) + jnp.log(1.0 + jnp.exp2(jnp.abs(zh) * (-LOG2E)))
            if causal is not None:
                sp = jnp.where(causal, sp, 0.0)
            hi = sp.astype(BF16)
            lo = (sp - hi.astype(F32)).astype(BF16)
            for u in range(n):
                sc_ref[h, u] = _dot(jnp.concatenate([hi[:, sub(u)], lo[:, sub(u)]], axis=1), uo)
        for h in range(N_HEADS):
            carry = carry_ref[h]
            ws = [None] * n
            for u in reversed(range(n)):
                w = jnp.exp(z_ref[h, :, sub(u)] - sc_ref[h, u, :, :blk] - carry)
                if causal is not None:
                    w = jnp.where(causal, w, 0.0)
                ws[u] = w.astype(BF16)
                carry = carry + sc_ref[h, u, :, blk:]
            carry_ref[h] = carry
            acc_ref[h] += _dot(jnp.concatenate(ws, axis=1), v_blk[:, col(h)])

    def alive():
        return jnp.min(carry_ref[...]) < -EXP_ZERO_BELOW

    row = lax.broadcasted_iota(jnp.int32, (blk, blk), 0)
    colm = lax.broadcasted_iota(jnp.int32, (blk, blk), 1)
    diag = pl.multiple_of((new_pad or 0) + i * blk, blk)
    visit(kn_ref[0, pl.ds(diag, blk), :], vn_ref[0, pl.ds(diag, blk), :], colm < row)

    def cond(c):
        j, live = c
        return jnp.logical_and(j >= 0, live)

    if new_pad is not None:
        def older_new(c):
            j, _ = c
            s = pl.multiple_of(diag - (j + 1) * 2 * blk, blk)
            visit(kn_ref[0, pl.ds(s, 2 * blk), :], vn_ref[0, pl.ds(s, 2 * blk), :], None)
            return j + 1, alive()

        def cond_new(c):
            j, live = c
            return jnp.logical_and(j < (i + 1) // 2, live)
        lax.while_loop(cond_new, older_new, (jnp.int32(0), alive()))

    if n_cache:
        def older_cache(c):
            j, _ = c
            s = pl.multiple_of(j * 2 * blk, 2 * blk)
            visit(kc_ref[0, pl.ds(s, 2 * blk), :].astype(BF16),
                  vc_ref[0, pl.ds(s, 2 * blk), :].astype(BF16), None)
            return j - 1, alive()
        lax.while_loop(cond, older_cache, (jnp.int32(n_cache - 1), alive()))

    for hp in range(HEAD_PAIRS):
        o_ref[0, :, hp * LANES:(hp + 1) * LANES] = jnp.where(
            lane_lo, acc_ref[2 * hp], acc_ref[2 * hp + 1])


def _cumsum_matrix(blk):
    j = np.arange(blk)[:, None]
    s = np.arange(blk)[None, :]
    u = (j >= s).astype(np.float32)
    half = np.concatenate([u, np.ones((blk, blk), np.float32)], axis=1)
    return jnp.asarray(np.concatenate([half, half], axis=0), dtype=BF16)


def _stick_breaking(q, k_new, v_new, k_cache=None, v_cache=None, *, blk):
    b, tq, _ = q.shape
    nq = tq // blk
    new_pad = k_new.shape[1] - tq if nq > 1 else None
    assert new_pad is None or new_pad >= blk
    n_cache = 0 if k_cache is None else k_cache.shape[1] // (2 * blk)
    full = lambda arr: pl.BlockSpec((1,) + arr.shape[1:], lambda bi, i: (bi, 0, 0))
    qblk = pl.BlockSpec((1, blk, WIDTH), lambda bi, i: (bi, i, 0))
    uo = _cumsum_matrix(blk)
    operands = [q, k_new, v_new]
    in_specs = [qblk, full(k_new), full(v_new)]
    if n_cache:
        operands += [k_cache, v_cache]
        in_specs += [full(k_cache), full(v_cache)]
    operands.append(uo)
    in_specs.append(pl.BlockSpec(uo.shape, lambda bi, i: (0, 0)))
    return pl.pallas_call(
        functools.partial(_sb_kernel, blk=blk, new_pad=new_pad, n_cache=n_cache),
        grid=(b, nq),
        in_specs=in_specs, out_specs=qblk,
        out_shape=jax.ShapeDtypeStruct((b, tq, WIDTH), F32),
        scratch_shapes=[pltpu.VMEM((N_HEADS, blk, LANES), BF16),
                        pltpu.VMEM((N_HEADS, blk, LANES), F32),
                        pltpu.VMEM((N_HEADS, blk, blk), F32),
                        pltpu.VMEM((N_HEADS, blk, 2 * blk), F32),
                        pltpu.VMEM((N_HEADS, 2, blk, 2 * blk), F32)],
        compiler_params=pltpu.CompilerParams(
            dimension_semantics=("arbitrary", "arbitrary"), vmem_limit_bytes=VMEM_LIMIT),
        name="stick_breaking",
    )(*operands)


def _bias_kernel(rb_ref, o_ref, *, g):
    rows, win = g * CHUNK, (LEFT_CHUNKS + g) * CHUNK
    n = 768
    npad = rb_ref.shape[1]
    jj = lax.broadcasted_iota(jnp.int32, (npad, n), 1)
    rr = lax.broadcasted_iota(jnp.int32, (npad, n), 0)
    idx = jnp.where(jj > win, 2 * REL_CLIP,
                    jnp.clip(BAND - jj, -REL_CLIP, REL_CLIP) + REL_CLIP)
    sel = (rr == idx).astype(BF16)
    rb = rb_ref[...]
    hi = rb.astype(BF16)
    r1 = rb - hi.astype(F32)
    mid = r1.astype(BF16)
    lo = (r1 - mid.astype(F32)).astype(BF16)
    line = _dot(hi, sel) + _dot(mid, sel) + _dot(lo, sel)
    tq = lax.broadcasted_iota(jnp.int32, (rows, win), 0)
    tk = lax.broadcasted_iota(jnp.int32, (rows, win), 1)
    off = tk - (tq // CHUNK) * CHUNK
    valid = jnp.logical_and(off >= 0, off < (LEFT_CHUNKS + 1) * CHUNK)
    for h in range(N_HEADS):
        tiled = jnp.broadcast_to(line[h:h + 1, :], (rows, n))
        rolled = pltpu.roll(tiled, 0, 1, stride=1, stride_axis=0)
        o_ref[h] = jnp.where(valid, rolled[:, :win], NEG_INF)


def _bias_table(rb_pad, g):
    rows, win = g * CHUNK, (LEFT_CHUNKS + g) * CHUNK
    return pl.pallas_call(
        functools.partial(_bias_kernel, g=g),
        out_shape=jax.ShapeDtypeStruct((N_HEADS, rows, win), F32),
        name="bias_table",
    )(rb_pad)


def _cb_kernel(q_ref, k_ref, v_ref, bias_ref, o_ref, s_ref, *, rows, win, pad_history):
    p = pl.program_id(1)
    start = pl.multiple_of(p * rows, rows)
    lane_lo = lax.broadcasted_iota(jnp.int32, (rows, LANES), 1) < HEAD_DIM
    if pad_history:
        key_ok = lax.broadcasted_iota(jnp.int32, (rows, win), 1) >= BAND - p * rows
    col = lambda h: slice((h // 2) * LANES, (h // 2 + 1) * LANES)

    def scores(h):
        q2 = q_ref[0, :, col(h)]
        zero = jnp.zeros_like(q2)
        qm = jnp.where(lane_lo, q2, zero) if h % 2 == 0 else jnp.where(lane_lo, zero, q2)
        return _dot_nt(qm, k_ref[0, pl.ds(start, win), col(h)])

    for h in range(N_HEADS):
        s = scores(h) + bias_ref[h]
        if pad_history:
            s = jnp.where(key_ok, s, NEG_INF)
        s_ref[h] = s
    outs = {}
    for h in range(N_HEADS):
        s = s_ref[h]
        e = jnp.exp(s - jnp.max(s, axis=-1, keepdims=True))
        l = jnp.sum(e, axis=-1, keepdims=True)
        outs[h] = _dot(e.astype(BF16), v_ref[0, pl.ds(start, win), col(h)]) / l
        if h % 2:
            o_ref[0, :, col(h)] = jnp.where(lane_lo, outs.pop(h - 1), outs.pop(h))


def _chunk_band(q, k_hist, v_hist, bias, *, g, pad_history):
    b, t, _ = q.shape
    rows, win = g * CHUNK, (LEFT_CHUNKS + g) * CHUNK
    full = lambda arr: pl.BlockSpec((1,) + arr.shape[1:], lambda bi, i: (bi, 0, 0))
    qblk = pl.BlockSpec((1, rows, WIDTH), lambda bi, i: (bi, i, 0))
    return pl.pallas_call(
        functools.partial(_cb_kernel, rows=rows, win=win, pad_history=pad_history),
        grid=(b, t // rows),
        in_specs=[qblk, full(k_hist), full(v_hist),
                  pl.BlockSpec(bias.shape, lambda bi, i: (0, 0, 0))],
        out_specs=qblk,
        out_shape=jax.ShapeDtypeStruct((b, t, WIDTH), F32),
        scratch_shapes=[pltpu.VMEM((N_HEADS, rows, win), F32)],
        compiler_params=pltpu.CompilerParams(
            dimension_semantics=("arbitrary", "arbitrary"), vmem_limit_bytes=VMEM_LIMIT),
        name="chunk_band",
    )(q, k_hist, v_hist, bias)


def _merge_kernel(x_ref, osb_ref, zsb_ref, ocb_ref, zcb_ref, gsb_ref, gcb_ref,
                  wsb_ref, wcb_ref, wout_ref, y_ref):
    zsb = zsb_ref[0]
    zcb = zcb_ref[0]
    a_sb = (osb_ref[0] * (zsb * _sigmoid(zsb))).astype(BF16)
    a_cb = (ocb_ref[0] * (zcb * _sigmoid(zcb))).astype(BF16)
    b_sb = _dot(a_sb, wsb_ref[...])
    b_cb = _dot(a_cb, wcb_ref[...])
    h = _sigmoid(gsb_ref[0]) * b_sb + _sigmoid(gcb_ref[0]) * b_cb
    y_ref[0] = x_ref[0] + _dot(h.astype(BF16), wout_ref[...])


def _merge(x, osb, zsb, ocb, zcb, gsb, gcb, wsb, wcb, wout, *, tm):
    b, t, _ = x.shape
    wide = pl.BlockSpec((1, tm, D_MODEL), lambda bi, i: (bi, i, 0))
    narrow = pl.BlockSpec((1, tm, WIDTH), lambda bi, i: (bi, i, 0))
    const = lambda arr: pl.BlockSpec(arr.shape, lambda bi, i: (0, 0))
    return pl.pallas_call(
        _merge_kernel,
        grid=(b, t // tm),
        in_specs=[wide, narrow, narrow, narrow, narrow, wide, wide,
                  const(wsb), const(wcb), const(wout)],
        out_specs=wide,
        out_shape=jax.ShapeDtypeStruct((b, t, D_MODEL), F32),
        compiler_params=pltpu.CompilerParams(
            dimension_semantics=("arbitrary", "arbitrary"), vmem_limit_bytes=VMEM_LIMIT),
        name="merge",
    )(x, osb, zsb, ocb, zcb, gsb, gcb, wsb, wcb, wout)


def _heads(a):
    b, t, _ = a.shape
    return a.reshape(b, t, N_HEADS, HEAD_DIM)


def kernel(x_prompt, x_sample, cache_sb_k, cache_sb_v, cache_cb_k, cache_cb_v, norm_w, w_in,
           q_norm_w, k_norm_w, rel_bias, w_proj_sb, w_proj_cb, w_out):
    depth = w_in.shape[0]
    b, t, _ = x_prompt.shape
    bs, tn, _ = x_sample.shape
    p = cache_sb_k.shape[2]
    r = cache_cb_k.shape[2]
    assert t % (2 * CHUNK) == 0 and tn == CHUNK and r == BAND and p % (2 * tn) == 0

    y_p, y_s = x_prompt, x_sample
    outs = [[] for _ in range(8)]
    for l in range(depth):
        nw = norm_w[l].reshape(1, D_MODEL)
        w_bf = w_in[l].astype(BF16)
        qnw = jnp.tile(q_norm_w[l], 2).reshape(1, LANES)
        knw = jnp.tile(k_norm_w[l], 2).reshape(1, LANES)
        rb_pad = jnp.pad(rel_bias[l], ((0, 0), (0, 384 - (2 * REL_CLIP + 1))))
        bias2 = _bias_table(rb_pad, 2)
        bias1 = bias2[:, :CHUNK, :(LEFT_CHUNKS + 1) * CHUNK]
        wsb = w_proj_sb[l].astype(BF16)
        wcb = w_proj_cb[l].astype(BF16)
        wout = w_out[l].astype(BF16)

        (sbq, sbk32, sbk16, sbv32, sbv16, sbz, cbq, cbk16, cbv16, cbk32, cbv32, cbz,
         gsb, gcb) = _project(y_p, nw, w_bf, qnw, knw, tm=512, pad_rows=BAND)
        o_sb = _stick_breaking(sbq, sbk16, sbv16, blk=128)
        o_cb = _chunk_band(cbq, cbk16, cbv16, bias2, g=2, pad_history=True)
        y_p_next = _merge(y_p, o_sb, sbz, o_cb, cbz, gsb, gcb, wsb, wcb, wout, tm=512)
        for lst, a in zip(outs[:4], (sbk32, sbv32, cbk32, cbv32)):
            lst.append(_heads(a))

        (sbq, sbk32, sbk16, sbv32, sbv16, sbz, cbq, cbk16, cbv16, cbk32, cbv32, cbz,
         gsb, gcb) = _project(y_s, nw, w_bf, qnw, knw, tm=tn, pad_rows=0)
        o_sb = _stick_breaking(sbq, sbk16, sbv16,
                               cache_sb_k[l].reshape(bs, p, WIDTH),
                               cache_sb_v[l].reshape(bs, p, WIDTH), blk=tn)
        k_hist = jnp.concatenate([cache_cb_k[l].reshape(bs, r, WIDTH).astype(BF16), cbk16], axis=1)
        v_hist = jnp.concatenate([cache_cb_v[l].reshape(bs, r, WIDTH).astype(BF16), cbv16], axis=1)
        o_cb = _chunk_band(cbq, k_hist, v_hist, bias1, g=1, pad_history=False)
        y_s_next = _merge(y_s, o_sb, sbz, o_cb, cbz, gsb, gcb, wsb, wcb, wout, tm=tn)
        for lst, a in zip(outs[4:], (sbk32, sbv32, cbk32, cbv32)):
            lst.append(_heads(a))

        y_p, y_s = y_p_next, y_s_next

    return (y_p, y_s) + tuple(jnp.stack(o) for o in outs)
```

```python
import functools

import jax
import jax.numpy as jnp
import numpy as np
from jax import lax
from jax.experimental import pallas as pl
from jax.experimental.pallas import tpu as pltpu

F32 = jnp.float32
BF16 = jnp.bfloat16

D_MODEL = 1024
CHUNK = 64
LEFT_CHUNKS = 8
BAND = LEFT_CHUNKS * CHUNK
N_HEADS = 8
HEAD_DIM = 64
WIDTH = N_HEADS * HEAD_DIM
REL_CLIP = 128
EPS = 1e-6
NEG_INF = -1e30
ATTN_SCALE = 0.125
IN_COLS = 4 * WIDTH + 4 * WIDTH + 2 * D_MODEL

LANES = 128
HEAD_PAIRS = N_HEADS // 2
EXP_ZERO_BELOW = -104.0
LOG2E = 1.4426950408889634
VMEM_LIMIT = 56 * 1024 * 1024


def _dot(a, b):
    return jnp.dot(a, b, preferred_element_type=F32)


def _dot_nt(a, b):
    return lax.dot_general(a, b, (((1,), (1,)), ((), ())), preferred_element_type=F32)


def _sigmoid(x):
    return 1.0 / (1.0 + jnp.exp(-x))


def _head_rms_norm(h, w128, lane_lo):
    sq = h * h
    s_lo = jnp.sum(jnp.where(lane_lo, sq, 0.0), axis=-1, keepdims=True)
    s_hi = jnp.sum(jnp.where(lane_lo, 0.0, sq), axis=-1, keepdims=True)
    r_lo = lax.rsqrt(s_lo * (1.0 / HEAD_DIM) + EPS)
    r_hi = lax.rsqrt(s_hi * (1.0 / HEAD_DIM) + EPS)
    return (h * jnp.where(lane_lo, r_lo, r_hi)) * w128


def _proj_kernel(x_ref, nw_ref, w_ref, qnw_ref, knw_ref,
                 sbq_ref, sbk32_ref, sbk16_ref, sbv32_ref, sbv16_ref, sbz_ref,
                 cbq_ref, cbk16_ref, cbv16_ref, cbk32_ref, cbv32_ref, cbz_ref,
                 gsb_ref, gcb_ref, *, pad_blocks):
    j = pl.program_id(1)

    if pad_blocks:
        @pl.when(j < pad_blocks)
        def _():
            for ref in (sbk16_ref, sbv16_ref, cbk16_ref, cbv16_ref):
                ref[...] = jnp.zeros(ref.shape, BF16)

    @pl.when(j >= pad_blocks)
    def _():
        x = x_ref[0]
        ms = jnp.mean(x * x, axis=-1, keepdims=True)
        xb = ((x * lax.rsqrt(ms + EPS)) * nw_ref[...]).astype(BF16)
        tm = x.shape[0]
        lane_lo = lax.broadcasted_iota(jnp.int32, (tm, LANES), 1) < HEAD_DIM

        def col(c0, width):
            return _dot(xb, w_ref[:, c0:c0 + width])

        sbq_ref[0] = (col(0, WIDTH) * ATTN_SCALE).astype(BF16)
        h = col(WIDTH, WIDTH)
        sbk32_ref[0] = h
        sbk16_ref[0] = h.astype(BF16)
        h = col(2 * WIDTH, WIDTH)
        sbv32_ref[0] = h
        sbv16_ref[0] = h.astype(BF16)
        sbz_ref[0] = col(3 * WIDTH, WIDTH).astype(BF16)

        hq_all = col(4 * WIDTH, WIDTH)
        hk_all = col(5 * WIDTH, WIDTH)
        for s in range(HEAD_PAIRS):
            sl = slice(s * LANES, (s + 1) * LANES)
            hq = _head_rms_norm(hq_all[:, sl], qnw_ref[...], lane_lo)
            cbq_ref[0, :, sl] = (hq * ATTN_SCALE).astype(BF16)
            hk = _head_rms_norm(hk_all[:, sl], knw_ref[...], lane_lo)
            cbk16_ref[0, :, sl] = hk.astype(BF16)
            cbk32_ref[0, :, sl] = hk

        h = col(6 * WIDTH, WIDTH)
        cbv16_ref[0] = h.astype(BF16)
        cbv32_ref[0] = h

        cbz_ref[0] = col(7 * WIDTH, WIDTH).astype(BF16)
        gsb_ref[0] = col(8 * WIDTH, D_MODEL).astype(BF16)
        gcb_ref[0] = col(8 * WIDTH + D_MODEL, D_MODEL).astype(BF16)


def _project(x, nw, w_bf, qnw128, knw128, *, tm, pad_rows, tail):
    b, t, _ = x.shape
    nt = t // tm
    pad_blocks = pad_rows // tm
    tail_blocks = tail // tm
    first_tail_step = pad_blocks + nt - tail_blocks

    def row(bi, j):
        return (bi, jnp.maximum(j - pad_blocks, 0), 0)

    def padded(bi, j):
        return (bi, j, 0)

    def tail_map(bi, j):
        return (bi, jnp.maximum(j - first_tail_step, 0), 0)

    def const(bi, j):
        return (0, 0)

    def out(width, dtype, rows=t):
        return jax.ShapeDtypeStruct((b, rows, width), dtype)

    blk = lambda width, imap: pl.BlockSpec((1, tm, width), imap)
    out_shape = [
        out(WIDTH, BF16), out(WIDTH, F32), out(WIDTH, BF16, t + pad_rows),
        out(WIDTH, F32), out(WIDTH, BF16, t + pad_rows), out(WIDTH, BF16),
        out(WIDTH, BF16), out(WIDTH, BF16, t + pad_rows), out(WIDTH, BF16, t + pad_rows),
        out(WIDTH, F32, tail), out(WIDTH, F32, tail), out(WIDTH, BF16),
        out(D_MODEL, BF16), out(D_MODEL, BF16),
    ]
    out_specs = [
        blk(WIDTH, row), blk(WIDTH, row), blk(WIDTH, padded),
        blk(WIDTH, row), blk(WIDTH, padded), blk(WIDTH, row),
        blk(WIDTH, row), blk(WIDTH, padded), blk(WIDTH, padded),
        blk(WIDTH, tail_map), blk(WIDTH, tail_map), blk(WIDTH, row),
        blk(D_MODEL, row), blk(D_MODEL, row),
    ]
    in_specs = [
        blk(D_MODEL, row),
        pl.BlockSpec((1, D_MODEL), const),
        pl.BlockSpec((D_MODEL, IN_COLS), const, pipeline_mode=pl.Buffered(1)),
        pl.BlockSpec((1, LANES), const),
        pl.BlockSpec((1, LANES), const),
    ]
    return pl.pallas_call(
        functools.partial(_proj_kernel, pad_blocks=pad_blocks),
        grid=(b, nt + pad_blocks),
        in_specs=in_specs, out_specs=out_specs, out_shape=out_shape,
        compiler_params=pltpu.CompilerParams(
            dimension_semantics=("arbitrary", "arbitrary"), vmem_limit_bytes=VMEM_LIMIT),
        name="project",
    )(x, nw, w_bf, qnw128, knw128)


def _sb_kernel(*refs, blk, new_pad, n_cache):
    if n_cache:
        (q_ref, kn_ref, vn_ref, kc_ref, vc_ref, uo_ref, o_ref,
         qm_ref, acc_ref, carry_ref, z_ref, sc_ref) = refs
    else:
        (q_ref, kn_ref, vn_ref, uo_ref, o_ref,
         qm_ref, acc_ref, carry_ref, z_ref, sc_ref) = refs
    i = pl.program_id(1)
    lane_lo = lax.broadcasted_iota(jnp.int32, (blk, LANES), 1) < HEAD_DIM
    uo = uo_ref[...]

    for hp in range(HEAD_PAIRS):
        q2 = q_ref[0, :, hp * LANES:(hp + 1) * LANES]
        zero = jnp.zeros_like(q2)
        qm_ref[2 * hp] = jnp.where(lane_lo, q2, zero)
        qm_ref[2 * hp + 1] = jnp.where(lane_lo, zero, q2)
    acc_ref[...] = jnp.zeros(acc_ref.shape, F32)
    carry_ref[...] = jnp.zeros(carry_ref.shape, F32)

    def visit(k_blk, v_blk, causal):
        n = k_blk.shape[0] // blk
        col = lambda h: slice((h // 2) * LANES, (h // 2 + 1) * LANES)
        sub = lambda u: slice(u * blk, (u + 1) * blk)
        for h in range(N_HEADS):
            z_ref[h, :, :n * blk] = _dot_nt(qm_ref[h], k_blk[:, col(h)])
        for h in range(N_HEADS):
            zh = z_ref[h, :, :n * blk]
            sp = jnp.maximum(zh, 0.0) + jnp.log(1.0 + jnp.exp2(jnp.abs(zh) * (-LOG2E)))
            if causal is not None:
                sp = jnp.where(causal, sp, 0.0)
            hi = sp.astype(BF16)
            lo = (sp - hi.astype(F32)).astype(BF16)
            for u in range(n):
                sc_ref[h, u] = _dot(jnp.concatenate([hi[:, sub(u)], lo[:, sub(u)]], axis=1), uo)
        for h in range(N_HEADS):
            carry = carry_ref[h]
            ws = [None] * n
            for u in reversed(range(n)):
                w = jnp.exp(z_ref[h, :, sub(u)] - sc_ref[h, u, :, :blk] - carry)
                if causal is not None:
                    w = jnp.where(causal, w, 0.0)
                ws[u] = w.astype(BF16)
                carry = carry + sc_ref[h, u, :, blk:]
            carry_ref[h] = carry
            acc_ref[h] += _dot(jnp.concatenate(ws, axis=1), v_blk[:, col(h)])

    def alive():
        return jnp.min(carry_ref[...]) < -EXP_ZERO_BELOW

    row = lax.broadcasted_iota(jnp.int32, (blk, blk), 0)
    colm = lax.broadcasted_iota(jnp.int32, (blk, blk), 1)
    diag = pl.multiple_of((new_pad or 0) + i * blk, blk)
    visit(kn_ref[0, pl.ds(diag, blk), :], vn_ref[0, pl.ds(diag, blk), :], colm < row)

    def cond(c):
        j, live = c
        return jnp.logical_and(j >= 0, live)

    if new_pad is not None:
        def older_new(c):
            j, _ = c
            s = pl.multiple_of(diag - (j + 1) * 2 * blk, blk)
            visit(kn_ref[0, pl.ds(s, 2 * blk), :], vn_ref[0, pl.ds(s, 2 * blk), :], None)
            return j + 1, alive()

        def cond_new(c):
            j, live = c
            return jnp.logical_and(j < (i + 1) // 2, live)
        lax.while_loop(cond_new, older_new, (jnp.int32(0), alive()))

    if n_cache:
        def older_cache(c):
            j, _ = c
            s = pl.multiple_of(j * 2 * blk, 2 * blk)
            visit(kc_ref[0, pl.ds(s, 2 * blk), :].astype(BF16),
                  vc_ref[0, pl.ds(s, 2 * blk), :].astype(BF16), None)
            return j - 1, alive()
        lax.while_loop(cond, older_cache, (jnp.int32(n_cache - 1), alive()))

    for hp in range(HEAD_PAIRS):
        o_ref[0, :, hp * LANES:(hp + 1) * LANES] = jnp.where(
            lane_lo, acc_ref[2 * hp], acc_ref[2 * hp + 1]).astype(o_ref.dtype)


def _cumsum_matrix(blk):
    j = np.arange(blk)[:, None]
    s = np.arange(blk)[None, :]
    u = (j >= s).astype(np.float32)
    half = np.concatenate([u, np.ones((blk, blk), np.float32)], axis=1)
    return jnp.asarray(np.concatenate([half, half], axis=0), dtype=BF16)


def _stick_breaking(q, k_new, v_new, k_cache=None, v_cache=None, *, blk):
    b, tq, _ = q.shape
    nq = tq // blk
    new_pad = k_new.shape[1] - tq if nq > 1 else None
    assert new_pad is None or new_pad >= blk
    n_cache = 0 if k_cache is None else k_cache.shape[1] // (2 * blk)
    full = lambda arr: pl.BlockSpec((1,) + arr.shape[1:], lambda bi, i: (bi, 0, 0))
    qblk = pl.BlockSpec((1, blk, WIDTH), lambda bi, i: (bi, i, 0))
    uo = _cumsum_matrix(blk)
    operands = [q, k_new, v_new]
    in_specs = [qblk, full(k_new), full(v_new)]
    if n_cache:
        operands += [k_cache, v_cache]
        in_specs += [full(k_cache), full(v_cache)]
    operands.append(uo)
    in_specs.append(pl.BlockSpec(uo.shape, lambda bi, i: (0, 0)))
    return pl.pallas_call(
        functools.partial(_sb_kernel, blk=blk, new_pad=new_pad, n_cache=n_cache),
        grid=(b, nq),
        in_specs=in_specs, out_specs=qblk,
        out_shape=jax.ShapeDtypeStruct((b, tq, WIDTH), BF16),
        scratch_shapes=[pltpu.VMEM((N_HEADS, blk, LANES), BF16),
                        pltpu.VMEM((N_HEADS, blk, LANES), F32),
                        pltpu.VMEM((N_HEADS, blk, blk), F32),
                        pltpu.VMEM((N_HEADS, blk, 2 * blk), F32),
                        pltpu.VMEM((N_HEADS, 2, blk, 2 * blk), F32)],
        compiler_params=pltpu.CompilerParams(
            dimension_semantics=("arbitrary", "arbitrary"), vmem_limit_bytes=VMEM_LIMIT),
        name="stick_breaking",
    )(*operands)


def _bias_kernel(rb_ref, o_ref, *, g):
    rows, win = g * CHUNK, (LEFT_CHUNKS + g) * CHUNK
    n = 768
    npad = rb_ref.shape[1]
    jj = lax.broadcasted_iota(jnp.int32, (npad, n), 1)
    rr = lax.broadcasted_iota(jnp.int32, (npad, n), 0)
    idx = jnp.where(jj > win, 2 * REL_CLIP,
                    jnp.clip(BAND - jj, -REL_CLIP, REL_CLIP) + REL_CLIP)
    sel = (rr == idx).astype(BF16)
    rb = rb_ref[...]
    hi = rb.astype(BF16)
    r1 = rb - hi.astype(F32)
    mid = r1.astype(BF16)
    lo = (r1 - mid.astype(F32)).astype(BF16)
    line = _dot(hi, sel) + _dot(mid, sel) + _dot(lo, sel)
    tq = lax.broadcasted_iota(jnp.int32, (rows, win), 0)
    tk = lax.broadcasted_iota(jnp.int32, (rows, win), 1)
    off = tk - (tq // CHUNK) * CHUNK
    valid = jnp.logical_and(off >= 0, off < (LEFT_CHUNKS + 1) * CHUNK)
    for h in range(N_HEADS):
        tiled = jnp.broadcast_to(line[h:h + 1, :], (rows, n))
        rolled = pltpu.roll(tiled, 0, 1, stride=1, stride_axis=0)
        o_ref[h] = jnp.where(valid, rolled[:, :win], NEG_INF)


def _bias_table(rb_pad, g):
    rows, win = g * CHUNK, (LEFT_CHUNKS + g) * CHUNK
    return pl.pallas_call(
        functools.partial(_bias_kernel, g=g),
        out_shape=jax.ShapeDtypeStruct((N_HEADS, rows, win), F32),
        name="bias_table",
    )(rb_pad)


def _cb_kernel(q_ref, k_ref, v_ref, bias_ref, o_ref, *scratch, rows, win, pad_history):
    s_refs, m_refs = scratch[:HEAD_PAIRS], scratch[HEAD_PAIRS:]
    p = pl.program_id(1)
    start = pl.multiple_of(p * rows, rows)
    lane_lo = lax.broadcasted_iota(jnp.int32, (rows, LANES), 1) < HEAD_DIM
    if pad_history:
        key_ok = lax.broadcasted_iota(jnp.int32, (2 * rows, win), 1) >= BAND - p * rows
    for hp in range(HEAD_PAIRS):
        sl = slice(hp * LANES, (hp + 1) * LANES)
        q2 = q_ref[0, :, sl]
        zero = jnp.zeros_like(q2)
        qm = jnp.concatenate([jnp.where(lane_lo, q2, zero), jnp.where(lane_lo, zero, q2)], axis=0)
        s = _dot_nt(qm, k_ref[0, pl.ds(start, win), sl]) + bias_ref[hp]
        if pad_history:
            s = jnp.where(key_ok, s, NEG_INF)
        s_refs[hp][0] = s
        m_refs[hp][0] = jnp.broadcast_to(jnp.max(s, axis=-1, keepdims=True), (2 * rows, LANES))
    zero_idx = jnp.minimum(p, 0)
    for hp in range(HEAD_PAIRS):
        sl = slice(hp * LANES, (hp + 1) * LANES)
        s = s_refs[hp][zero_idx]
        if win % LANES == 0:
            m = jnp.concatenate([m_refs[hp][zero_idx]] * (win // LANES), axis=1)
        else:
            m = m_refs[hp][zero_idx][:, :1]
        e = jnp.exp(s - m)
        l = jnp.sum(e, axis=-1, keepdims=True)
        o = _dot(e.astype(BF16), v_ref[0, pl.ds(start, win), sl]) / l
        o_ref[0, :, sl] = jnp.where(lane_lo, o[:rows], o[rows:]).astype(o_ref.dtype)


def _chunk_band(q, k_hist, v_hist, bias, *, g, pad_history):
    b, t, _ = q.shape
    rows, win = g * CHUNK, (LEFT_CHUNKS + g) * CHUNK
    full = lambda arr: pl.BlockSpec((1,) + arr.shape[1:], lambda bi, i: (bi, 0, 0))
    qblk = pl.BlockSpec((1, rows, WIDTH), lambda bi, i: (bi, i, 0))
    bias = bias.reshape(HEAD_PAIRS, 2 * rows, win)
    return pl.pallas_call(
        functools.partial(_cb_kernel, rows=rows, win=win, pad_history=pad_history),
        grid=(b, t // rows),
        in_specs=[qblk, full(k_hist), full(v_hist),
                  pl.BlockSpec(bias.shape, lambda bi, i: (0, 0, 0))],
        out_specs=qblk,
        out_shape=jax.ShapeDtypeStruct((b, t, WIDTH), BF16),
        scratch_shapes=([pltpu.VMEM((1, 2 * rows, win), F32)] * HEAD_PAIRS
                        + [pltpu.VMEM((1, 2 * rows, LANES), F32)] * HEAD_PAIRS),
        compiler_params=pltpu.CompilerParams(
            dimension_semantics=("arbitrary", "arbitrary"), vmem_limit_bytes=VMEM_LIMIT),
        name="chunk_band",
    )(q, k_hist, v_hist, bias)


def _merge_kernel(x_ref, osb_ref, zsb_ref, ocb_ref, zcb_ref, gsb_ref, gcb_ref,
                  wsb_ref, wcb_ref, wout_ref, y_ref):
    zsb = zsb_ref[0].astype(F32)
    zcb = zcb_ref[0].astype(F32)
    a_sb = (osb_ref[0].astype(F32) * (zsb * _sigmoid(zsb))).astype(BF16)
    a_cb = (ocb_ref[0].astype(F32) * (zcb * _sigmoid(zcb))).astype(BF16)
    b_sb = _dot(a_sb, wsb_ref[...])
    b_cb = _dot(a_cb, wcb_ref[...])
    h = (_sigmoid(gsb_ref[0].astype(F32)) * b_sb
         + _sigmoid(gcb_ref[0].astype(F32)) * b_cb)
    y_ref[0] = x_ref[0] + _dot(h.astype(BF16), wout_ref[...])


def _merge(x, osb, zsb, ocb, zcb, gsb, gcb, wsb, wcb, wout, *, tm):
    b, t, _ = x.shape
    wide = pl.BlockSpec((1, tm, D_MODEL), lambda bi, i: (bi, i, 0))
    narrow = pl.BlockSpec((1, tm, WIDTH), lambda bi, i: (bi, i, 0))
    const = lambda arr: pl.BlockSpec(arr.shape, lambda bi, i: (0, 0))
    return pl.pallas_call(
        _merge_kernel,
        grid=(b, t // tm),
        in_specs=[wide, narrow, narrow, narrow, narrow, wide, wide,
                  const(wsb), const(wcb), const(wout)],
        out_specs=wide,
        out_shape=jax.ShapeDtypeStruct((b, t, D_MODEL), F32),
        compiler_params=pltpu.CompilerParams(
            dimension_semantics=("arbitrary", "arbitrary"), vmem_limit_bytes=VMEM_LIMIT),
        name="merge",
    )(x, osb, zsb, ocb, zcb, gsb, gcb, wsb, wcb, wout)


def _heads(a):
    b, t, _ = a.shape
    return a.reshape(b, t, N_HEADS, HEAD_DIM)


def kernel(x_prompt, x_sample, cache_sb_k, cache_sb_v, cache_cb_k, cache_cb_v, norm_w, w_in,
           q_norm_w, k_norm_w, rel_bias, w_proj_sb, w_proj_cb, w_out):
    depth = w_in.shape[0]
    b, t, _ = x_prompt.shape
    bs, tn, _ = x_sample.shape
    p = cache_sb_k.shape[2]
    r = cache_cb_k.shape[2]
    assert t % (2 * CHUNK) == 0 and tn == CHUNK and r == BAND and p % (2 * tn) == 0

    y_p, y_s = x_prompt, x_sample
    outs = [[] for _ in range(8)]
    for l in range(depth):
        nw = norm_w[l].reshape(1, D_MODEL)
        w_bf = w_in[l].astype(BF16)
        qnw = jnp.tile(q_norm_w[l], 2).reshape(1, LANES)
        knw = jnp.tile(k_norm_w[l], 2).reshape(1, LANES)
        rb_pad = jnp.pad(rel_bias[l], ((0, 0), (0, 384 - (2 * REL_CLIP + 1))))
        bias2 = _bias_table(rb_pad, 2)
        bias1 = bias2[:, :CHUNK, :(LEFT_CHUNKS + 1) * CHUNK]
        wsb = w_proj_sb[l].astype(BF16)
        wcb = w_proj_cb[l].astype(BF16)
        wout = w_out[l].astype(BF16)

        (sbq, sbk32, sbk16, sbv32, sbv16, sbz, cbq, cbk16, cbv16, cbk32, cbv32, cbz,
         gsb, gcb) = _project(y_p, nw, w_bf, qnw, knw, tm=512, pad_rows=BAND, tail=min(BAND, t))
        o_sb = _stick_breaking(sbq, sbk16, sbv16, blk=128)
        o_cb = _chunk_band(cbq, cbk16, cbv16, bias2, g=2, pad_history=True)
        y_p_next = _merge(y_p, o_sb, sbz, o_cb, cbz, gsb, gcb, wsb, wcb, wout, tm=512)
        for lst, a in zip(outs[:4], (sbk32, sbv32, cbk32, cbv32)):
            lst.append(_heads(a))

        proj = _project(y_s.reshape(1, bs * tn, D_MODEL), nw, w_bf, qnw, knw,
                        tm=bs * tn, pad_rows=0, tail=bs * tn)
        (sbq, sbk32, sbk16, sbv32, sbv16, sbz, cbq, cbk16, cbv16, cbk32, cbv32, cbz,
         gsb, gcb) = [a.reshape(bs, tn, a.shape[-1]) for a in proj]
        o_sb = _stick_breaking(sbq, sbk16, sbv16,
                               cache_sb_k[l].reshape(bs, p, WIDTH),
                               cache_sb_v[l].reshape(bs, p, WIDTH), blk=tn)
        k_hist = jnp.concatenate([cache_cb_k[l].reshape(bs, r, WIDTH).astype(BF16), cbk16], axis=1)
        v_hist = jnp.concatenate([cache_cb_v[l].reshape(bs, r, WIDTH).astype(BF16), cbv16], axis=1)
        o_cb = _chunk_band(cbq, k_hist, v_hist, bias1, g=1, pad_history=False)
        flat = lambda a: a.reshape(1, bs * tn, a.shape[-1])
        y_s_next = _merge(flat(y_s), flat(o_sb), flat(sbz), flat(o_cb), flat(cbz), flat(gsb),
                          flat(gcb), wsb, wcb, wout, tm=bs * tn).reshape(bs, tn, D_MODEL)
        for lst, a in zip(outs[4:], (sbk32, sbv32, cbk32, cbv32)):
            lst.append(_heads(a))

        y_p, y_s = y_p_next, y_s_next

    return (y_p, y_s) + tuple(jnp.stack(o) for o in outs)
```

```python
import functools

import jax
import jax.numpy as jnp
import numpy as np
from jax import lax
from jax.experimental import pallas as pl
from jax.experimental.pallas import tpu as pltpu

F32 = jnp.float32
BF16 = jnp.bfloat16

D_MODEL = 1024
CHUNK = 64
LEFT_CHUNKS = 8
BAND = LEFT_CHUNKS * CHUNK
N_HEADS = 8
HEAD_DIM = 64
WIDTH = N_HEADS * HEAD_DIM
REL_CLIP = 128
EPS = 1e-6
NEG_INF = -1e30
ATTN_SCALE = 0.125
IN_COLS = 4 * WIDTH + 4 * WIDTH + 2 * D_MODEL

LANES = 128
HEAD_PAIRS = N_HEADS // 2
EXP_ZERO_BELOW = -104.0
LOG2E = 1.4426950408889634
VMEM_LIMIT = 56 * 1024 * 1024


def _dot(a, b):
    return jnp.dot(a, b, preferred_element_type=F32)


def _dot_nt(a, b):
    return lax.dot_general(a, b, (((1,), (1,)), ((), ())), preferred_element_type=F32)


def _sigmoid(x):
    return 1.0 / (1.0 + jnp.exp(-x))


def _head_rms_norm(h, w128, lane_lo):
    sq = h * h
    s_lo = jnp.sum(jnp.where(lane_lo, sq, 0.0), axis=-1, keepdims=True)
    s_hi = jnp.sum(jnp.where(lane_lo, 0.0, sq), axis=-1, keepdims=True)
    r_lo = lax.rsqrt(s_lo * (1.0 / HEAD_DIM) + EPS)
    r_hi = lax.rsqrt(s_hi * (1.0 / HEAD_DIM) + EPS)
    return (h * jnp.where(lane_lo, r_lo, r_hi)) * w128


def _proj_kernel(x_ref, nw_ref, w_ref, qnw_ref, knw_ref,
                 sbq_ref, sbk32_ref, sbk16_ref, sbv32_ref, sbv16_ref, sbz_ref,
                 cbq_ref, cbk16_ref, cbv16_ref, cbk32_ref, cbv32_ref, cbz_ref,
                 gsb_ref, gcb_ref, *, pad_blocks):
    j = pl.program_id(1)

    if pad_blocks:
        @pl.when(j < pad_blocks)
        def _():
            for ref in (sbk16_ref, sbv16_ref, cbk16_ref, cbv16_ref):
                ref[...] = jnp.zeros(ref.shape, BF16)

    @pl.when(j >= pad_blocks)
    def _():
        x = x_ref[0]
        ms = jnp.mean(x * x, axis=-1, keepdims=True)
        xb = ((x * lax.rsqrt(ms + EPS)) * nw_ref[...]).astype(BF16)
        tm = x.shape[0]
        lane_lo = lax.broadcasted_iota(jnp.int32, (tm, LANES), 1) < HEAD_DIM

        def col(c0, width):
            return _dot(xb, w_ref[:, c0:c0 + width])

        sbq_ref[0] = (col(0, WIDTH) * ATTN_SCALE).astype(BF16)
        h = col(WIDTH, WIDTH)
        sbk32_ref[0] = h
        sbk16_ref[0] = h.astype(BF16)
        h = col(2 * WIDTH, WIDTH)
        sbv32_ref[0] = h
        sbv16_ref[0] = h.astype(BF16)
        sbz_ref[0] = col(3 * WIDTH, WIDTH).astype(BF16)

        hq_all = col(4 * WIDTH, WIDTH)
        hk_all = col(5 * WIDTH, WIDTH)
        for s in range(HEAD_PAIRS):
            sl = slice(s * LANES, (s + 1) * LANES)
            hq = _head_rms_norm(hq_all[:, sl], qnw_ref[...], lane_lo)
            cbq_ref[0, :, sl] = (hq * (ATTN_SCALE * LOG2E)).astype(BF16)
            hk = _head_rms_norm(hk_all[:, sl], knw_ref[...], lane_lo)
            cbk16_ref[0, :, sl] = hk.astype(BF16)
            cbk32_ref[0, :, sl] = hk

        h = col(6 * WIDTH, WIDTH)
        cbv16_ref[0] = h.astype(BF16)
        cbv32_ref[0] = h

        cbz_ref[0] = col(7 * WIDTH, WIDTH).astype(BF16)
        gsb_ref[0] = col(8 * WIDTH, D_MODEL).astype(BF16)
        gcb_ref[0] = col(8 * WIDTH + D_MODEL, D_MODEL).astype(BF16)


def _project(x, nw, w_bf, qnw128, knw128, *, tm, pad_rows, tail):
    b, t, _ = x.shape
    nt = t // tm
    pad_blocks = pad_rows // tm
    tail_blocks = tail // tm
    first_tail_step = pad_blocks + nt - tail_blocks

    def row(bi, j):
        return (bi, jnp.maximum(j - pad_blocks, 0), 0)

    def padded(bi, j):
        return (bi, j, 0)

    def tail_map(bi, j):
        return (bi, jnp.maximum(j - first_tail_step, 0), 0)

    def const(bi, j):
        return (0, 0)

    def out(width, dtype, rows=t):
        return jax.ShapeDtypeStruct((b, rows, width), dtype)

    blk = lambda width, imap: pl.BlockSpec((1, tm, width), imap)
    out_shape = [
        out(WIDTH, BF16), out(WIDTH, F32), out(WIDTH, BF16, t + pad_rows),
        out(WIDTH, F32), out(WIDTH, BF16, t + pad_rows), out(WIDTH, BF16),
        out(WIDTH, BF16), out(WIDTH, BF16, t + pad_rows), out(WIDTH, BF16, t + pad_rows),
        out(WIDTH, F32, tail), out(WIDTH, F32, tail), out(WIDTH, BF16),
        out(D_MODEL, BF16), out(D_MODEL, BF16),
    ]
    out_specs = [
        blk(WIDTH, row), blk(WIDTH, row), blk(WIDTH, padded),
        blk(WIDTH, row), blk(WIDTH, padded), blk(WIDTH, row),
        blk(WIDTH, row), blk(WIDTH, padded), blk(WIDTH, padded),
        blk(WIDTH, tail_map), blk(WIDTH, tail_map), blk(WIDTH, row),
        blk(D_MODEL, row), blk(D_MODEL, row),
    ]
    in_specs = [
        blk(D_MODEL, row),
        pl.BlockSpec((1, D_MODEL), const),
        pl.BlockSpec((D_MODEL, IN_COLS), const, pipeline_mode=pl.Buffered(1)),
        pl.BlockSpec((1, LANES), const),
        pl.BlockSpec((1, LANES), const),
    ]
    return pl.pallas_call(
        functools.partial(_proj_kernel, pad_blocks=pad_blocks),
        grid=(b, nt + pad_blocks),
        in_specs=in_specs, out_specs=out_specs, out_shape=out_shape,
        compiler_params=pltpu.CompilerParams(
            dimension_semantics=("arbitrary", "arbitrary"), vmem_limit_bytes=VMEM_LIMIT),
        name="project",
    )(x, nw, w_bf, qnw128, knw128)


def _sb_kernel(*refs, blk, new_pad, n_cache):
    if n_cache:
        (q_ref, kn_ref, vn_ref, kc_ref, vc_ref, uo_ref, o_ref,
         qm_ref, acc_ref, carry_ref, z_ref, sc_ref) = refs
    else:
        (q_ref, kn_ref, vn_ref, uo_ref, o_ref,
         qm_ref, acc_ref, carry_ref, z_ref, sc_ref) = refs
    i = pl.program_id(1)
    lane_lo = lax.broadcasted_iota(jnp.int32, (blk, LANES), 1) < HEAD_DIM
    uo = uo_ref[...]

    for hp in range(HEAD_PAIRS):
        q2 = q_ref[0, :, hp * LANES:(hp + 1) * LANES]
        zero = jnp.zeros_like(q2)
        qm_ref[2 * hp] = jnp.where(lane_lo, q2, zero)
        qm_ref[2 * hp + 1] = jnp.where(lane_lo, zero, q2)
    acc_ref[...] = jnp.zeros(acc_ref.shape, F32)
    carry_ref[...] = jnp.zeros(carry_ref.shape, F32)

    def visit(k_blk, v_blk, causal):
        n = k_blk.shape[0] // blk
        col = lambda h: slice((h // 2) * LANES, (h // 2 + 1) * LANES)
        sub = lambda u: slice(u * blk, (u + 1) * blk)
        masked = lambda u: causal is not None and u == n - 1
        for h in range(N_HEADS):
            z_ref[h, :, :n * blk] = _dot_nt(qm_ref[h], k_blk[:, col(h)])
        for h in range(N_HEADS):
            zh = z_ref[h, :, :n * blk]
            sp = jnp.maximum(zh, 0.0) + jnp.log(1.0 + jnp.exp2(jnp.abs(zh) * (-LOG2E)))
            for u in range(n):
                spu = jnp.where(causal, sp[:, sub(u)], 0.0) if masked(u) else sp[:, sub(u)]
                hi = spu.astype(BF16)
                lo = (spu - hi.astype(F32)).astype(BF16)
                sc_ref[h, u] = _dot(jnp.concatenate([hi, lo], axis=1), uo)
        for h in range(N_HEADS):
            carry = carry_ref[h]
            ws = [None] * n
            for u in reversed(range(n)):
                w = jnp.exp(z_ref[h, :, sub(u)] - sc_ref[h, u, :, :blk] - carry)
                if masked(u):
                    w = jnp.where(causal, w, 0.0)
                ws[u] = w.astype(BF16)
                carry = carry + sc_ref[h, u, :, blk:]
            carry_ref[h] = carry
            acc_ref[h] += _dot(jnp.concatenate(ws, axis=1), v_blk[:, col(h)])
        return jnp.min(carry_ref[...]) < -EXP_ZERO_BELOW

    row = lax.broadcasted_iota(jnp.int32, (blk, blk), 0)
    colm = lax.broadcasted_iota(jnp.int32, (blk, blk), 1)
    causal = colm < row
    diag = pl.multiple_of((new_pad or 0) + i * blk, blk)

    if new_pad is not None:
        first = pl.multiple_of(diag - 2 * blk, blk)
        live = visit(kn_ref[0, pl.ds(first, 3 * blk), :], vn_ref[0, pl.ds(first, 3 * blk), :],
                     causal)

        def older_new(c):
            j, _ = c
            s = pl.multiple_of(diag - (j + 1) * 2 * blk, blk)
            return j + 1, visit(kn_ref[0, pl.ds(s, 2 * blk), :],
                                vn_ref[0, pl.ds(s, 2 * blk), :], None)

        def cond_new(c):
            j, live = c
            return jnp.logical_and(j < (i + 1) // 2, live)
        lax.while_loop(cond_new, older_new, (jnp.int32(1), live))
    else:
        top = (n_cache - 1) * 2 * blk
        live = visit(
              jnp.concatenate([kc_ref[0, top:top + 2 * blk, :],
                               kn_ref[0, pl.ds(diag, blk), :]], axis=0),
              jnp.concatenate([vc_ref[0, top:top + 2 * blk, :],
                               vn_ref[0, pl.ds(diag, blk), :]], axis=0), causal)

        def older_cache(c):
            j, _ = c
            s = pl.multiple_of(j * 2 * blk, 2 * blk)
            return j - 1, visit(kc_ref[0, pl.ds(s, 2 * blk), :],
                                vc_ref[0, pl.ds(s, 2 * blk), :], None)

        def cond_cache(c):
            j, live = c
            return jnp.logical_and(j >= 0, live)
        lax.while_loop(cond_cache, older_cache, (jnp.int32(n_cache - 2), live))

    for hp in range(HEAD_PAIRS):
        o_ref[0, :, hp * LANES:(hp + 1) * LANES] = jnp.where(
            lane_lo, acc_ref[2 * hp], acc_ref[2 * hp + 1]).astype(o_ref.dtype)


def _cumsum_matrix(blk):
    j = np.arange(blk)[:, None]
    s = np.arange(blk)[None, :]
    u = (j >= s).astype(np.float32)
    half = np.concatenate([u, np.ones((blk, blk), np.float32)], axis=1)
    return jnp.asarray(np.concatenate([half, half], axis=0), dtype=BF16)


def _stick_breaking(q, k_new, v_new, k_cache=None, v_cache=None, *, blk):
    b, tq, _ = q.shape
    nq = tq // blk
    new_pad = k_new.shape[1] - tq if nq > 1 else None
    assert (new_pad is None) == (k_cache is not None)
    assert new_pad is None or new_pad >= 2 * blk
    n_cache = 0 if k_cache is None else k_cache.shape[1] // (2 * blk)
    full = lambda arr: pl.BlockSpec((1,) + arr.shape[1:], lambda bi, i: (bi, 0, 0))
    qblk = pl.BlockSpec((1, blk, WIDTH), lambda bi, i: (bi, i, 0))
    uo = _cumsum_matrix(blk)
    operands = [q, k_new, v_new]
    in_specs = [qblk, full(k_new), full(v_new)]
    if n_cache:
        operands += [k_cache, v_cache]
        in_specs += [full(k_cache), full(v_cache)]
    operands.append(uo)
    in_specs.append(pl.BlockSpec(uo.shape, lambda bi, i: (0, 0)))
    return pl.pallas_call(
        functools.partial(_sb_kernel, blk=blk, new_pad=new_pad, n_cache=n_cache),
        grid=(b, nq),
        in_specs=in_specs, out_specs=qblk,
        out_shape=jax.ShapeDtypeStruct((b, tq, WIDTH), BF16),
        scratch_shapes=[pltpu.VMEM((N_HEADS, blk, LANES), BF16),
                        pltpu.VMEM((N_HEADS, blk, LANES), F32),
                        pltpu.VMEM((N_HEADS, blk, blk), F32),
                        pltpu.VMEM((N_HEADS, blk, 3 * blk), F32),
                        pltpu.VMEM((N_HEADS, 3, blk, 2 * blk), F32)],
        compiler_params=pltpu.CompilerParams(
            dimension_semantics=("arbitrary", "arbitrary"), vmem_limit_bytes=VMEM_LIMIT),
        name="stick_breaking",
    )(*operands)


def _bias_kernel(rb_ref, o_ref, *, g):
    rows, win = g * CHUNK, (LEFT_CHUNKS + g) * CHUNK
    n = 768
    npad = rb_ref.shape[1]
    jj = lax.broadcasted_iota(jnp.int32, (npad, n), 1)
    rr = lax.broadcasted_iota(jnp.int32, (npad, n), 0)
    idx = jnp.where(jj > win, 2 * REL_CLIP,
                    jnp.clip(BAND - jj, -REL_CLIP, REL_CLIP) + REL_CLIP)
    sel = (rr == idx).astype(BF16)
    rb = rb_ref[...]
    hi = rb.astype(BF16)
    r1 = rb - hi.astype(F32)
    mid = r1.astype(BF16)
    lo = (r1 - mid.astype(F32)).astype(BF16)
    line = _dot(hi, sel) + _dot(mid, sel) + _dot(lo, sel)
    line = line * LOG2E
    tq = lax.broadcasted_iota(jnp.int32, (rows, win), 0)
    tk = lax.broadcasted_iota(jnp.int32, (rows, win), 1)
    off = tk - (tq // CHUNK) * CHUNK
    valid = jnp.logical_and(off >= 0, off < (LEFT_CHUNKS + 1) * CHUNK)
    for h in range(N_HEADS):
        tiled = jnp.broadcast_to(line[h:h + 1, :], (rows, n))
        rolled = pltpu.roll(tiled, 0, 1, stride=1, stride_axis=0)
        o_ref[h] = jnp.where(valid, rolled[:, :win], NEG_INF)


def _bias_table(rb_pad, g):
    rows, win = g * CHUNK, (LEFT_CHUNKS + g) * CHUNK
    return pl.pallas_call(
        functools.partial(_bias_kernel, g=g),
        out_shape=jax.ShapeDtypeStruct((N_HEADS, rows, win), F32),
        name="bias_table",
    )(rb_pad)


def _cb_kernel(q_ref, k_ref, v_ref, bias_ref, o_ref, *scratch, rows, win, pad_history):
    s_refs, m_refs = scratch[:HEAD_PAIRS], scratch[HEAD_PAIRS:]
    p = pl.program_id(1)
    start = pl.multiple_of(p * rows, rows)
    lane_lo = lax.broadcasted_iota(jnp.int32, (rows, LANES), 1) < HEAD_DIM
    zero_idx = jnp.minimum(p, 0)

    def body(mask_pad):
        if mask_pad:
            key_ok = lax.broadcasted_iota(jnp.int32, (2 * rows, win), 1) >= BAND - p * rows
        for hp in range(HEAD_PAIRS):
            sl = slice(hp * LANES, (hp + 1) * LANES)
            q2 = q_ref[0, :, sl]
            zero = jnp.zeros_like(q2)
            qm = jnp.concatenate([jnp.where(lane_lo, q2, zero), jnp.where(lane_lo, zero, q2)],
                                 axis=0)
            s = _dot_nt(qm, k_ref[0, pl.ds(start, win), sl]) + bias_ref[hp]
            if mask_pad:
                s = jnp.where(key_ok, s, NEG_INF)
            s_refs[hp][0] = s
            m_refs[hp][0] = jnp.broadcast_to(jnp.max(s, axis=-1, keepdims=True),
                                             (2 * rows, LANES))
        for hp in range(HEAD_PAIRS):
            sl = slice(hp * LANES, (hp + 1) * LANES)
            s = s_refs[hp][zero_idx]
            if win % LANES == 0:
                m = jnp.concatenate([m_refs[hp][zero_idx]] * (win // LANES), axis=1)
            else:
                m = m_refs[hp][zero_idx][:, :1]
            e = jnp.exp2(s - m)
            l = jnp.sum(e, axis=-1, keepdims=True)
            o = _dot(e.astype(BF16), v_ref[0, pl.ds(start, win), sl]) / l
            o_ref[0, :, sl] = jnp.where(lane_lo, o[:rows], o[rows:]).astype(o_ref.dtype)

    if pad_history:
        pad_tiles = BAND // rows
        pl.when(p < pad_tiles)(functools.partial(body, True))
        pl.when(p >= pad_tiles)(functools.partial(body, False))
    else:
        body(False)


def _chunk_band(q, k_hist, v_hist, bias, *, g, pad_history):
    b, t, _ = q.shape
    rows, win = g * CHUNK, (LEFT_CHUNKS + g) * CHUNK
    full = lambda arr: pl.BlockSpec((1,) + arr.shape[1:], lambda bi, i: (bi, 0, 0))
    qblk = pl.BlockSpec((1, rows, WIDTH), lambda bi, i: (bi, i, 0))
    bias = bias.reshape(HEAD_PAIRS, 2 * rows, win)
    return pl.pallas_call(
        functools.partial(_cb_kernel, rows=rows, win=win, pad_history=pad_history),
        grid=(b, t // rows),
        in_specs=[qblk, full(k_hist), full(v_hist),
                  pl.BlockSpec(bias.shape, lambda bi, i: (0, 0, 0))],
        out_specs=qblk,
        out_shape=jax.ShapeDtypeStruct((b, t, WIDTH), BF16),
        scratch_shapes=([pltpu.VMEM((1, 2 * rows, win), F32)] * HEAD_PAIRS
                        + [pltpu.VMEM((1, 2 * rows, LANES), F32)] * HEAD_PAIRS),
        compiler_params=pltpu.CompilerParams(
            dimension_semantics=("arbitrary", "arbitrary"), vmem_limit_bytes=VMEM_LIMIT),
        name="chunk_band",
    )(q, k_hist, v_hist, bias)


def _merge_kernel(x_ref, osb_ref, zsb_ref, ocb_ref, zcb_ref, gsb_ref, gcb_ref,
                  wsb_ref, wcb_ref, wout_ref, y_ref):
    zsb = zsb_ref[0].astype(F32)
    zcb = zcb_ref[0].astype(F32)
    a_sb = (osb_ref[0].astype(F32) * (zsb * _sigmoid(zsb))).astype(BF16)
    a_cb = (ocb_ref[0].astype(F32) * (zcb * _sigmoid(zcb))).astype(BF16)
    b_sb = _dot(a_sb, wsb_ref[...])
    b_cb = _dot(a_cb, wcb_ref[...])
    h = (_sigmoid(gsb_ref[0].astype(F32)) * b_sb
         + _sigmoid(gcb_ref[0].astype(F32)) * b_cb)
    y_ref[0] = x_ref[0] + _dot(h.astype(BF16), wout_ref[...])


def _merge(x, osb, zsb, ocb, zcb, gsb, gcb, wsb, wcb, wout, *, tm):
    b, t, _ = x.shape
    wide = pl.BlockSpec((1, tm, D_MODEL), lambda bi, i: (bi, i, 0))
    narrow = pl.BlockSpec((1, tm, WIDTH), lambda bi, i: (bi, i, 0))
    const = lambda arr: pl.BlockSpec(arr.shape, lambda bi, i: (0, 0))
    return pl.pallas_call(
        _merge_kernel,
        grid=(b, t // tm),
        in_specs=[wide, narrow, narrow, narrow, narrow, wide, wide,
                  const(wsb), const(wcb), const(wout)],
        out_specs=wide,
        out_shape=jax.ShapeDtypeStruct((b, t, D_MODEL), F32),
        compiler_params=pltpu.CompilerParams(
            dimension_semantics=("arbitrary", "arbitrary"), vmem_limit_bytes=VMEM_LIMIT),
        name="merge",
    )(x, osb, zsb, ocb, zcb, gsb, gcb, wsb, wcb, wout)


def _heads(a):
    b, t, _ = a.shape
    return a.reshape(b, t, N_HEADS, HEAD_DIM)


def kernel(x_prompt, x_sample, cache_sb_k, cache_sb_v, cache_cb_k, cache_cb_v, norm_w, w_in,
           q_norm_w, k_norm_w, rel_bias, w_proj_sb, w_proj_cb, w_out):
    depth = w_in.shape[0]
    b, t, _ = x_prompt.shape
    bs, tn, _ = x_sample.shape
    p = cache_sb_k.shape[2]
    r = cache_cb_k.shape[2]
    assert t % (2 * CHUNK) == 0 and tn == CHUNK and r == BAND and p % (2 * tn) == 0

    y_p, y_s = x_prompt, x_sample
    outs = [[] for _ in range(8)]
    for l in range(depth):
        nw = norm_w[l].reshape(1, D_MODEL)
        w_bf = w_in[l].astype(BF16)
        qnw = jnp.tile(q_norm_w[l], 2).reshape(1, LANES)
        knw = jnp.tile(k_norm_w[l], 2).reshape(1, LANES)
        rb_pad = jnp.pad(rel_bias[l], ((0, 0), (0, 384 - (2 * REL_CLIP + 1))))
        bias2 = _bias_table(rb_pad, 2)
        bias1 = bias2[:, :CHUNK, :(LEFT_CHUNKS + 1) * CHUNK]
        wsb = w_proj_sb[l].astype(BF16)
        wcb = w_proj_cb[l].astype(BF16)
        wout = w_out[l].astype(BF16)

        (sbq, sbk32, sbk16, sbv32, sbv16, sbz, cbq, cbk16, cbv16, cbk32, cbv32, cbz,
         gsb, gcb) = _project(y_p, nw, w_bf, qnw, knw, tm=512, pad_rows=BAND, tail=min(BAND, t))
        o_sb = _stick_breaking(sbq, sbk16, sbv16, blk=128)
        o_cb = _chunk_band(cbq, cbk16, cbv16, bias2, g=2, pad_history=True)
        y_p_next = _merge(y_p, o_sb, sbz, o_cb, cbz, gsb, gcb, wsb, wcb, wout, tm=512)
        for lst, a in zip(outs[:4], (sbk32, sbv32, cbk32, cbv32)):
            lst.append(_heads(a))

        proj = _project(y_s.reshape(1, bs * tn, D_MODEL), nw, w_bf, qnw, knw,
                        tm=bs * tn, pad_rows=0, tail=bs * tn)
        (sbq, sbk32, sbk16, sbv32, sbv16, sbz, cbq, cbk16, cbv16, cbk32, cbv32, cbz,
         gsb, gcb) = [a.reshape(bs, tn, a.shape[-1]) for a in proj]
        o_sb = _stick_breaking(sbq, sbk16, sbv16,
                               cache_sb_k[l].reshape(bs, p, WIDTH).astype(BF16),
                               cache_sb_v[l].reshape(bs, p, WIDTH).astype(BF16), blk=tn)
        k_hist = jnp.concatenate([cache_cb_k[l].reshape(bs, r, WIDTH).astype(BF16), cbk16], axis=1)
        v_hist = jnp.concatenate([cache_cb_v[l].reshape(bs, r, WIDTH).astype(BF16), cbv16], axis=1)
        o_cb = _chunk_band(cbq, k_hist, v_hist, bias1, g=1, pad_history=False)
        flat = lambda a: a.reshape(1, bs * tn, a.shape[-1])
        y_s_next = _merge(flat(y_s), flat(o_sb), flat(sbz), flat(o_cb), flat(cbz), flat(gsb),
                          flat(gcb), wsb, wcb, wout, tm=bs * tn).reshape(bs, tn, D_MODEL)
        for lst, a in zip(outs[4:], (sbk32, sbv32, cbk32, cbv32)):
            lst.append(_heads(a))

        y_p, y_s = y_p_next, y_s_next

    return (y_p, y_s) + tuple(jnp.stack(o) for o in outs)
```

```python
import functools

import jax
import jax.numpy as jnp
import numpy as np
from jax import lax
from jax.experimental import pallas as pl
from jax.experimental.pallas import tpu as pltpu

F32 = jnp.float32
BF16 = jnp.bfloat16

D_MODEL = 1024
CHUNK = 64
LEFT_CHUNKS = 8
BAND = LEFT_CHUNKS * CHUNK
N_HEADS = 8
HEAD_DIM = 64
WIDTH = N_HEADS * HEAD_DIM
REL_CLIP = 128
EPS = 1e-6
NEG_INF = -1e30
ATTN_SCALE = 0.125
IN_COLS = 4 * WIDTH + 4 * WIDTH + 2 * D_MODEL

LANES = 128
HEAD_PAIRS = N_HEADS // 2
EXP_ZERO_BELOW = -104.0
LOG2E = 1.4426950408889634
VMEM_LIMIT = 56 * 1024 * 1024


def _dot(a, b):
    return jnp.dot(a, b, preferred_element_type=F32)


def _dot_nt(a, b):
    return lax.dot_general(a, b, (((1,), (1,)), ((), ())), preferred_element_type=F32)


def _sigmoid(x):
    return 1.0 / (1.0 + jnp.exp(-x))


def _head_rms_norm(h, w128, lane_lo):
    sq = h * h
    s_lo = jnp.sum(jnp.where(lane_lo, sq, 0.0), axis=-1, keepdims=True)
    s_hi = jnp.sum(jnp.where(lane_lo, 0.0, sq), axis=-1, keepdims=True)
    r_lo = lax.rsqrt(s_lo * (1.0 / HEAD_DIM) + EPS)
    r_hi = lax.rsqrt(s_hi * (1.0 / HEAD_DIM) + EPS)
    return (h * jnp.where(lane_lo, r_lo, r_hi)) * w128


def _proj_kernel(x_ref, nw_ref, w_ref, qnw_ref, knw_ref,
                 sbq_ref, sbk32_ref, sbk16_ref, sbv32_ref, sbv16_ref, sbz_ref,
                 cbq_ref, cbk16_ref, cbv16_ref, cbk32_ref, cbv32_ref, cbz_ref,
                 gsb_ref, gcb_ref, *, pad_blocks):
    j = pl.program_id(1)

    if pad_blocks:
        @pl.when(j < pad_blocks)
        def _():
            for ref in (sbk16_ref, sbv16_ref, cbk16_ref, cbv16_ref):
                ref[...] = jnp.zeros(ref.shape, BF16)

    @pl.when(j >= pad_blocks)
    def _():
        x = x_ref[0]
        ms = jnp.mean(x * x, axis=-1, keepdims=True)
        xb = ((x * lax.rsqrt(ms + EPS)) * nw_ref[...]).astype(BF16)
        tm = x.shape[0]
        lane_lo = lax.broadcasted_iota(jnp.int32, (tm, LANES), 1) < HEAD_DIM

        def col(c0, width):
            return _dot(xb, w_ref[:, c0:c0 + width])

        sbq_ref[0] = (col(0, WIDTH) * ATTN_SCALE).astype(BF16)
        h = col(WIDTH, WIDTH)
        sbk32_ref[0] = h
        sbk16_ref[0] = h.astype(BF16)
        h = col(2 * WIDTH, WIDTH)
        sbv32_ref[0] = h
        sbv16_ref[0] = h.astype(BF16)
        sbz_ref[0] = col(3 * WIDTH, WIDTH).astype(BF16)

        hq_all = col(4 * WIDTH, WIDTH)
        hk_all = col(5 * WIDTH, WIDTH)
        for s in range(HEAD_PAIRS):
            sl = slice(s * LANES, (s + 1) * LANES)
            hq = _head_rms_norm(hq_all[:, sl], qnw_ref[...], lane_lo)
            cbq_ref[0, :, sl] = (hq * (ATTN_SCALE * LOG2E)).astype(BF16)
            hk = _head_rms_norm(hk_all[:, sl], knw_ref[...], lane_lo)
            cbk16_ref[0, :, sl] = hk.astype(BF16)
            cbk32_ref[0, :, sl] = hk

        h = col(6 * WIDTH, WIDTH)
        cbv16_ref[0] = h.astype(BF16)
        cbv32_ref[0] = h

        cbz_ref[0] = col(7 * WIDTH, WIDTH).astype(BF16)
        gsb_ref[0] = col(8 * WIDTH, D_MODEL).astype(BF16)
        gcb_ref[0] = col(8 * WIDTH + D_MODEL, D_MODEL).astype(BF16)


def _project(x, nw, w_bf, qnw128, knw128, *, tm, pad_rows, tail):
    b, t, _ = x.shape
    nt = t // tm
    pad_blocks = pad_rows // tm
    tail_blocks = tail // tm
    first_tail_step = pad_blocks + nt - tail_blocks

    def row(bi, j):
        return (bi, jnp.maximum(j - pad_blocks, 0), 0)

    def padded(bi, j):
        return (bi, j, 0)

    def tail_map(bi, j):
        return (bi, jnp.maximum(j - first_tail_step, 0), 0)

    def const(bi, j):
        return (0, 0)

    def out(width, dtype, rows=t):
        return jax.ShapeDtypeStruct((b, rows, width), dtype)

    blk = lambda width, imap: pl.BlockSpec((1, tm, width), imap)
    out_shape = [
        out(WIDTH, BF16), out(WIDTH, F32), out(WIDTH, BF16, t + pad_rows),
        out(WIDTH, F32), out(WIDTH, BF16, t + pad_rows), out(WIDTH, BF16),
        out(WIDTH, BF16), out(WIDTH, BF16, t + pad_rows), out(WIDTH, BF16, t + pad_rows),
        out(WIDTH, F32, tail), out(WIDTH, F32, tail), out(WIDTH, BF16),
        out(D_MODEL, BF16), out(D_MODEL, BF16),
    ]
    out_specs = [
        blk(WIDTH, row), blk(WIDTH, row), blk(WIDTH, padded),
        blk(WIDTH, row), blk(WIDTH, padded), blk(WIDTH, row),
        blk(WIDTH, row), blk(WIDTH, padded), blk(WIDTH, padded),
        blk(WIDTH, tail_map), blk(WIDTH, tail_map), blk(WIDTH, row),
        blk(D_MODEL, row), blk(D_MODEL, row),
    ]
    in_specs = [
        blk(D_MODEL, row),
        pl.BlockSpec((1, D_MODEL), const),
        pl.BlockSpec((D_MODEL, IN_COLS), const, pipeline_mode=pl.Buffered(1)),
        pl.BlockSpec((1, LANES), const),
        pl.BlockSpec((1, LANES), const),
    ]
    return pl.pallas_call(
        functools.partial(_proj_kernel, pad_blocks=pad_blocks),
        grid=(b, nt + pad_blocks),
        in_specs=in_specs, out_specs=out_specs, out_shape=out_shape,
        compiler_params=pltpu.CompilerParams(
            dimension_semantics=("arbitrary", "arbitrary"), vmem_limit_bytes=VMEM_LIMIT),
        name="project",
    )(x, nw, w_bf, qnw128, knw128)


def _sb_kernel(*refs, blk, new_pad, n_cache):
    if n_cache:
        (q_ref, kn_ref, vn_ref, kc_ref, vc_ref, uo_ref, o_ref,
         qm_ref, acc_ref, carry_ref, z_ref, sc_ref) = refs
    else:
        (q_ref, kn_ref, vn_ref, uo_ref, o_ref,
         qm_ref, acc_ref, carry_ref, z_ref, sc_ref) = refs
    i = pl.program_id(1)
    lane_lo = lax.broadcasted_iota(jnp.int32, (blk, LANES), 1) < HEAD_DIM
    uo = uo_ref[...]

    head = lambda h: slice(h * HEAD_DIM, (h + 1) * HEAD_DIM)
    pair = lambda h: slice((h // 2) * LANES, (h // 2 + 1) * LANES)
    if new_pad is not None:
        for hp in range(HEAD_PAIRS):
            q2 = q_ref[0, :, pair(2 * hp)]
            zero = jnp.zeros_like(q2)
            qm_ref[2 * hp] = jnp.where(lane_lo, q2, zero)
            qm_ref[2 * hp + 1] = jnp.where(lane_lo, zero, q2)
    else:
        for h in range(N_HEADS):
            qm_ref[h] = q_ref[0, :, head(h)]
    acc_ref[...] = jnp.zeros(acc_ref.shape, F32)
    carry_ref[...] = jnp.zeros(carry_ref.shape, F32)

    def visit(get_k, get_v, n, causal):
        sub = lambda u: slice(u * blk, (u + 1) * blk)
        masked = lambda u: causal is not None and u == n - 1
        for h in range(N_HEADS):
            z_ref[h, :, :n * blk] = _dot_nt(qm_ref[h], get_k(h))
        for h in range(N_HEADS):
            zh = z_ref[h, :, :n * blk]
            sp = jnp.maximum(zh, 0.0) + jnp.log(1.0 + jnp.exp2(jnp.abs(zh) * (-LOG2E)))
            for u in range(n):
                spu = jnp.where(causal, sp[:, sub(u)], 0.0) if masked(u) else sp[:, sub(u)]
                hi = spu.astype(BF16)
                lo = (spu - hi.astype(F32)).astype(BF16)
                sc_ref[h, u] = _dot(jnp.concatenate([hi, lo], axis=1), uo)
        for h in range(N_HEADS):
            carry = carry_ref[h]
            ws = [None] * n
            for u in reversed(range(n)):
                w = jnp.exp(z_ref[h, :, sub(u)] - sc_ref[h, u, :, :blk] - carry)
                if masked(u):
                    w = jnp.where(causal, w, 0.0)
                ws[u] = w.astype(BF16)
                carry = carry + sc_ref[h, u, :, blk:]
            carry_ref[h] = carry
            acc_ref[h] += _dot(jnp.concatenate(ws, axis=1), get_v(h))
        return jnp.min(carry_ref[...]) < -EXP_ZERO_BELOW

    row = lax.broadcasted_iota(jnp.int32, (blk, blk), 0)
    colm = lax.broadcasted_iota(jnp.int32, (blk, blk), 1)
    causal = colm < row
    diag = pl.multiple_of((new_pad or 0) + i * blk, blk)

    if new_pad is not None:
        def new_rows(ref, s, n):
            tile = ref[0, pl.ds(s, n * blk), :]
            return lambda h: tile[:, pair(h)]

        first = pl.multiple_of(diag - 2 * blk, blk)
        live = visit(new_rows(kn_ref, first, 3), new_rows(vn_ref, first, 3), 3, causal)

        def older_new(c):
            j, _ = c
            s = pl.multiple_of(diag - (j + 1) * 2 * blk, blk)
            return j + 1, visit(new_rows(kn_ref, s, 2), new_rows(vn_ref, s, 2), 2, None)

        def cond_new(c):
            j, live = c
            return jnp.logical_and(j < (i + 1) // 2, live)
        lax.while_loop(cond_new, older_new, (jnp.int32(1), live))

        for hp in range(HEAD_PAIRS):
            o_ref[0, :, pair(2 * hp)] = jnp.where(
                lane_lo, acc_ref[2 * hp], acc_ref[2 * hp + 1]).astype(o_ref.dtype)
    else:
        def cache_rows(ref, s):
            return lambda h: ref[0, pl.ds(s * N_HEADS + h, 2 * blk, stride=N_HEADS), :].astype(BF16)

        def with_new(cache_fn, new_ref):
            return lambda h: jnp.concatenate([cache_fn(h), new_ref[0, :, head(h)]], axis=0)

        top = (n_cache - 1) * 2 * blk
        live = visit(with_new(cache_rows(kc_ref, top), kn_ref),
                     with_new(cache_rows(vc_ref, top), vn_ref), 3, causal)

        def older_cache(c):
            j, _ = c
            s = j * 2 * blk
            return j - 1, visit(cache_rows(kc_ref, s), cache_rows(vc_ref, s), 2, None)

        def cond_cache(c):
            j, live = c
            return jnp.logical_and(j >= 0, live)
        lax.while_loop(cond_cache, older_cache, (jnp.int32(n_cache - 2), live))

        for h in range(N_HEADS):
            o_ref[0, :, head(h)] = acc_ref[h].astype(o_ref.dtype)


def _cumsum_matrix(blk):
    j = np.arange(blk)[:, None]
    s = np.arange(blk)[None, :]
    u = (j >= s).astype(np.float32)
    half = np.concatenate([u, np.ones((blk, blk), np.float32)], axis=1)
    return jnp.asarray(np.concatenate([half, half], axis=0), dtype=BF16)


def _stick_breaking(q, k_new, v_new, k_cache=None, v_cache=None, *, blk):
    b, tq, _ = q.shape
    nq = tq // blk
    new_pad = k_new.shape[1] - tq if nq > 1 else None
    assert (new_pad is None) == (k_cache is not None)
    assert new_pad is None or new_pad >= 2 * blk
    n_cache = 0 if k_cache is None else k_cache.shape[1] // (N_HEADS * 2 * blk)
    d = HEAD_DIM if n_cache else LANES
    full = lambda arr: pl.BlockSpec((1,) + arr.shape[1:], lambda bi, i: (bi, 0, 0))
    qblk = pl.BlockSpec((1, blk, WIDTH), lambda bi, i: (bi, i, 0))
    uo = _cumsum_matrix(blk)
    operands = [q, k_new, v_new]
    in_specs = [qblk, full(k_new), full(v_new)]
    if n_cache:
        operands += [k_cache, v_cache]
        in_specs += [full(k_cache), full(v_cache)]
    operands.append(uo)
    in_specs.append(pl.BlockSpec(uo.shape, lambda bi, i: (0, 0)))
    return pl.pallas_call(
        functools.partial(_sb_kernel, blk=blk, new_pad=new_pad, n_cache=n_cache),
        grid=(b, nq),
        in_specs=in_specs, out_specs=qblk,
        out_shape=jax.ShapeDtypeStruct((b, tq, WIDTH), BF16),
        scratch_shapes=[pltpu.VMEM((N_HEADS, blk, d), BF16),
                        pltpu.VMEM((N_HEADS, blk, d), F32),
                        pltpu.VMEM((N_HEADS, blk, blk), F32),
                        pltpu.VMEM((N_HEADS, blk, 3 * blk), F32),
                        pltpu.VMEM((N_HEADS, 3, blk, 2 * blk), F32)],
        compiler_params=pltpu.CompilerParams(
            dimension_semantics=("arbitrary", "arbitrary"), vmem_limit_bytes=VMEM_LIMIT),
        name="stick_breaking",
    )(*operands)


def _bias_kernel(rb_ref, o_ref, *, g):
    rows, win = g * CHUNK, (LEFT_CHUNKS + g) * CHUNK
    n = 768
    npad = rb_ref.shape[1]
    jj = lax.broadcasted_iota(jnp.int32, (npad, n), 1)
    rr = lax.broadcasted_iota(jnp.int32, (npad, n), 0)
    idx = jnp.where(jj > win, 2 * REL_CLIP,
                    jnp.clip(BAND - jj, -REL_CLIP, REL_CLIP) + REL_CLIP)
    sel = (rr == idx).astype(BF16)
    rb = rb_ref[...]
    hi = rb.astype(BF16)
    r1 = rb - hi.astype(F32)
    mid = r1.astype(BF16)
    lo = (r1 - mid.astype(F32)).astype(BF16)
    line = _dot(hi, sel) + _dot(mid, sel) + _dot(lo, sel)
    line = line * LOG2E
    tq = lax.broadcasted_iota(jnp.int32, (rows, win), 0)
    tk = lax.broadcasted_iota(jnp.int32, (rows, win), 1)
    off = tk - (tq // CHUNK) * CHUNK
    valid = jnp.logical_and(off >= 0, off < (LEFT_CHUNKS + 1) * CHUNK)
    for h in range(N_HEADS):
        tiled = jnp.broadcast_to(line[h:h + 1, :], (rows, n))
        rolled = pltpu.roll(tiled, 0, 1, stride=1, stride_axis=0)
        o_ref[h] = jnp.where(valid, rolled[:, :win], NEG_INF)


def _bias_table(rb_pad, g):
    rows, win = g * CHUNK, (LEFT_CHUNKS + g) * CHUNK
    return pl.pallas_call(
        functools.partial(_bias_kernel, g=g),
        out_shape=jax.ShapeDtypeStruct((N_HEADS, rows, win), F32),
        name="bias_table",
    )(rb_pad)


def _cb_kernel(q_ref, k_ref, v_ref, bias_ref, o_ref, *scratch, rows, win, pad_history):
    s_refs, m_refs = scratch[:HEAD_PAIRS], scratch[HEAD_PAIRS:]
    p = pl.program_id(1)
    start = pl.multiple_of(p * rows, rows)
    lane_lo = lax.broadcasted_iota(jnp.int32, (rows, LANES), 1) < HEAD_DIM
    zero_idx = jnp.minimum(p, 0)

    def body(mask_pad):
        if mask_pad:
            key_ok = lax.broadcasted_iota(jnp.int32, (2 * rows, win), 1) >= BAND - p * rows
        for hp in range(HEAD_PAIRS):
            sl = slice(hp * LANES, (hp + 1) * LANES)
            q2 = q_ref[0, :, sl]
            zero = jnp.zeros_like(q2)
            qm = jnp.concatenate([jnp.where(lane_lo, q2, zero), jnp.where(lane_lo, zero, q2)],
                                 axis=0)
            s = _dot_nt(qm, k_ref[0, pl.ds(start, win), sl]) + bias_ref[hp]
            if mask_pad:
                s = jnp.where(key_ok, s, NEG_INF)
            s_refs[hp][0] = s
            m_refs[hp][0] = jnp.broadcast_to(jnp.max(s, axis=-1, keepdims=True),
                                             (2 * rows, LANES))
        for hp in range(HEAD_PAIRS):
            sl = slice(hp * LANES, (hp + 1) * LANES)
            s = s_refs[hp][zero_idx]
            if win % LANES == 0:
                m = jnp.concatenate([m_refs[hp][zero_idx]] * (win // LANES), axis=1)
            else:
                m = m_refs[hp][zero_idx][:, :1]
            e = jnp.exp2(s - m)
            l = jnp.sum(e, axis=-1, keepdims=True)
            o = _dot(e.astype(BF16), v_ref[0, pl.ds(start, win), sl]) / l
            o_ref[0, :, sl] = jnp.where(lane_lo, o[:rows], o[rows:]).astype(o_ref.dtype)

    if pad_history:
        pad_tiles = BAND // rows
        pl.when(p < pad_tiles)(functools.partial(body, True))
        pl.when(p >= pad_tiles)(functools.partial(body, False))
    else:
        body(False)


def _chunk_band(q, k_hist, v_hist, bias, *, g, pad_history):
    b, t, _ = q.shape
    rows, win = g * CHUNK, (LEFT_CHUNKS + g) * CHUNK
    full = lambda arr: pl.BlockSpec((1,) + arr.shape[1:], lambda bi, i: (bi, 0, 0))
    qblk = pl.BlockSpec((1, rows, WIDTH), lambda bi, i: (bi, i, 0))
    bias = bias.reshape(HEAD_PAIRS, 2 * rows, win)
    return pl.pallas_call(
        functools.partial(_cb_kernel, rows=rows, win=win, pad_history=pad_history),
        grid=(b, t // rows),
        in_specs=[qblk, full(k_hist), full(v_hist),
                  pl.BlockSpec(bias.shape, lambda bi, i: (0, 0, 0))],
        out_specs=qblk,
        out_shape=jax.ShapeDtypeStruct((b, t, WIDTH), BF16),
        scratch_shapes=([pltpu.VMEM((1, 2 * rows, win), F32)] * HEAD_PAIRS
                        + [pltpu.VMEM((1, 2 * rows, LANES), F32)] * HEAD_PAIRS),
        compiler_params=pltpu.CompilerParams(
            dimension_semantics=("arbitrary", "arbitrary"), vmem_limit_bytes=VMEM_LIMIT),
        name="chunk_band",
    )(q, k_hist, v_hist, bias)


def _merge_kernel(x_ref, osb_ref, zsb_ref, ocb_ref, zcb_ref, gsb_ref, gcb_ref,
                  wsb_ref, wcb_ref, wout_ref, y_ref):
    zsb = zsb_ref[0].astype(F32)
    zcb = zcb_ref[0].astype(F32)
    a_sb = (osb_ref[0].astype(F32) * (zsb * _sigmoid(zsb))).astype(BF16)
    a_cb = (ocb_ref[0].astype(F32) * (zcb * _sigmoid(zcb))).astype(BF16)
    b_sb = _dot(a_sb, wsb_ref[...])
    b_cb = _dot(a_cb, wcb_ref[...])
    h = (_sigmoid(gsb_ref[0].astype(F32)) * b_sb
         + _sigmoid(gcb_ref[0].astype(F32)) * b_cb)
    y_ref[0] = x_ref[0] + _dot(h.astype(BF16), wout_ref[...])


def _merge(x, osb, zsb, ocb, zcb, gsb, gcb, wsb, wcb, wout, *, tm):
    b, t, _ = x.shape
    wide = pl.BlockSpec((1, tm, D_MODEL), lambda bi, i: (bi, i, 0))
    narrow = pl.BlockSpec((1, tm, WIDTH), lambda bi, i: (bi, i, 0))
    const = lambda arr: pl.BlockSpec(arr.shape, lambda bi, i: (0, 0))
    return pl.pallas_call(
        _merge_kernel,
        grid=(b, t // tm),
        in_specs=[wide, narrow, narrow, narrow, narrow, wide, wide,
                  const(wsb), const(wcb), const(wout)],
        out_specs=wide,
        out_shape=jax.ShapeDtypeStruct((b, t, D_MODEL), F32),
        compiler_params=pltpu.CompilerParams(
            dimension_semantics=("arbitrary", "arbitrary"), vmem_limit_bytes=VMEM_LIMIT),
        name="merge",
    )(x, osb, zsb, ocb, zcb, gsb, gcb, wsb, wcb, wout)


def _heads(a):
    b, t, _ = a.shape
    return a.reshape(b, t, N_HEADS, HEAD_DIM)


def kernel(x_prompt, x_sample, cache_sb_k, cache_sb_v, cache_cb_k, cache_cb_v, norm_w, w_in,
           q_norm_w, k_norm_w, rel_bias, w_proj_sb, w_proj_cb, w_out):
    depth = w_in.shape[0]
    b, t, _ = x_prompt.shape
    bs, tn, _ = x_sample.shape
    p = cache_sb_k.shape[2]
    r = cache_cb_k.shape[2]
    assert t % (2 * CHUNK) == 0 and tn == CHUNK and r == BAND and p % (2 * tn) == 0

    y_p, y_s = x_prompt, x_sample
    outs = [[] for _ in range(8)]
    for l in range(depth):
        nw = norm_w[l].reshape(1, D_MODEL)
        w_bf = w_in[l].astype(BF16)
        qnw = jnp.tile(q_norm_w[l], 2).reshape(1, LANES)
        knw = jnp.tile(k_norm_w[l], 2).reshape(1, LANES)
        rb_pad = jnp.pad(rel_bias[l], ((0, 0), (0, 384 - (2 * REL_CLIP + 1))))
        bias2 = _bias_table(rb_pad, 2)
        bias1 = bias2[:, :CHUNK, :(LEFT_CHUNKS + 1) * CHUNK]
        wsb = w_proj_sb[l].astype(BF16)
        wcb = w_proj_cb[l].astype(BF16)
        wout = w_out[l].astype(BF16)

        (sbq, sbk32, sbk16, sbv32, sbv16, sbz, cbq, cbk16, cbv16, cbk32, cbv32, cbz,
         gsb, gcb) = _project(y_p, nw, w_bf, qnw, knw, tm=512, pad_rows=BAND, tail=min(BAND, t))
        o_sb = _stick_breaking(sbq, sbk16, sbv16, blk=128)
        o_cb = _chunk_band(cbq, cbk16, cbv16, bias2, g=2, pad_history=True)
        y_p_next = _merge(y_p, o_sb, sbz, o_cb, cbz, gsb, gcb, wsb, wcb, wout, tm=512)
        for lst, a in zip(outs[:4], (sbk32, sbv32, cbk32, cbv32)):
            lst.append(_heads(a))

        proj = _project(y_s.reshape(1, bs * tn, D_MODEL), nw, w_bf, qnw, knw,
                        tm=bs * tn, pad_rows=0, tail=bs * tn)
        (sbq, sbk32, sbk16, sbv32, sbv16, sbz, cbq, cbk16, cbv16, cbk32, cbv32, cbz,
         gsb, gcb) = [a.reshape(bs, tn, a.shape[-1]) for a in proj]
        o_sb = _stick_breaking(sbq, sbk16, sbv16,
                               cache_sb_k[l].reshape(bs, p * N_HEADS, HEAD_DIM),
                               cache_sb_v[l].reshape(bs, p * N_HEADS, HEAD_DIM), blk=tn)
        k_hist = jnp.concatenate([cache_cb_k[l].reshape(bs, r, WIDTH).astype(BF16), cbk16], axis=1)
        v_hist = jnp.concatenate([cache_cb_v[l].reshape(bs, r, WIDTH).astype(BF16), cbv16], axis=1)
        o_cb = _chunk_band(cbq, k_hist, v_hist, bias1, g=1, pad_history=False)
        flat = lambda a: a.reshape(1, bs * tn, a.shape[-1])
        y_s_next = _merge(flat(y_s), flat(o_sb), flat(sbz), flat(o_cb), flat(cbz), flat(gsb),
                          flat(gcb), wsb, wcb, wout, tm=bs * tn).reshape(bs, tn, D_MODEL)
        for lst, a in zip(outs[4:], (sbk32, sbv32, cbk32, cbv32)):
            lst.append(_heads(a))

        y_p, y_s = y_p_next, y_s_next

    return (y_p, y_s) + tuple(jnp.stack(o) for o in outs)
```

```python
import functools

import jax
import jax.numpy as jnp
import numpy as np
from jax import lax
from jax.experimental import pallas as pl
from jax.experimental.pallas import tpu as pltpu

F32 = jnp.float32
BF16 = jnp.bfloat16

D_MODEL = 1024
CHUNK = 64
LEFT_CHUNKS = 8
BAND = LEFT_CHUNKS * CHUNK
N_HEADS = 8
HEAD_DIM = 64
WIDTH = N_HEADS * HEAD_DIM
REL_CLIP = 128
EPS = 1e-6
NEG_INF = -1e30
ATTN_SCALE = 0.125
IN_COLS = 4 * WIDTH + 4 * WIDTH + 2 * D_MODEL

LANES = 128
HEAD_PAIRS = N_HEADS // 2
EXP_ZERO_BELOW = -104.0
LOG2E = 1.4426950408889634
VMEM_LIMIT = 56 * 1024 * 1024


def _dot(a, b):
    return jnp.dot(a, b, preferred_element_type=F32)


def _dot_nt(a, b):
    return lax.dot_general(a, b, (((1,), (1,)), ((), ())), preferred_element_type=F32)


def _sigmoid(x):
    return 1.0 / (1.0 + jnp.exp(-x))


def _head_rms_norm(h, w128, lane_lo):
    sq = h * h
    s_lo = jnp.sum(jnp.where(lane_lo, sq, 0.0), axis=-1, keepdims=True)
    s_hi = jnp.sum(jnp.where(lane_lo, 0.0, sq), axis=-1, keepdims=True)
    r_lo = lax.rsqrt(s_lo * (1.0 / HEAD_DIM) + EPS)
    r_hi = lax.rsqrt(s_hi * (1.0 / HEAD_DIM) + EPS)
    return (h * jnp.where(lane_lo, r_lo, r_hi)) * w128


def _proj_kernel(x_ref, nw_ref, w_ref, qnw_ref, knw_ref,
                 sbq_ref, sbk32_ref, sbk16_ref, sbv32_ref, sbv16_ref, sbz_ref,
                 cbq_ref, cbk16_ref, cbv16_ref, cbk32_ref, cbv32_ref, cbz_ref,
                 gsb_ref, gcb_ref, *, pad_blocks):
    j = pl.program_id(1)

    if pad_blocks:
        @pl.when(j < pad_blocks)
        def _():
            for ref in (sbk16_ref, sbv16_ref, cbk16_ref, cbv16_ref):
                ref[...] = jnp.zeros(ref.shape, BF16)

    @pl.when(j >= pad_blocks)
    def _():
        x = x_ref[0]
        ms = jnp.mean(x * x, axis=-1, keepdims=True)
        xb = ((x * lax.rsqrt(ms + EPS)) * nw_ref[...]).astype(BF16)
        tm = x.shape[0]
        lane_lo = lax.broadcasted_iota(jnp.int32, (tm, LANES), 1) < HEAD_DIM

        def col(c0, width):
            return _dot(xb, w_ref[:, c0:c0 + width])

        sbq_ref[0] = (col(0, WIDTH) * ATTN_SCALE).astype(BF16)
        h = col(WIDTH, WIDTH)
        sbk32_ref[0] = h
        sbk16_ref[0] = h.astype(BF16)
        h = col(2 * WIDTH, WIDTH)
        sbv32_ref[0] = h
        sbv16_ref[0] = h.astype(BF16)
        sbz_ref[0] = col(3 * WIDTH, WIDTH).astype(BF16)

        hq_all = col(4 * WIDTH, WIDTH)
        hk_all = col(5 * WIDTH, WIDTH)
        for s in range(HEAD_PAIRS):
            sl = slice(s * LANES, (s + 1) * LANES)
            hq = _head_rms_norm(hq_all[:, sl], qnw_ref[...], lane_lo)
            cbq_ref[0, :, sl] = (hq * (ATTN_SCALE * LOG2E)).astype(BF16)
            hk = _head_rms_norm(hk_all[:, sl], knw_ref[...], lane_lo)
            cbk16_ref[0, :, sl] = hk.astype(BF16)
            cbk32_ref[0, :, sl] = hk

        h = col(6 * WIDTH, WIDTH)
        cbv16_ref[0] = h.astype(BF16)
        cbv32_ref[0] = h

        cbz_ref[0] = col(7 * WIDTH, WIDTH).astype(BF16)
        gsb_ref[0] = col(8 * WIDTH, D_MODEL).astype(BF16)
        gcb_ref[0] = col(8 * WIDTH + D_MODEL, D_MODEL).astype(BF16)


def _project(x, nw, w_bf, qnw128, knw128, *, tm, pad_rows, tail):
    b, t, _ = x.shape
    nt = t // tm
    pad_blocks = pad_rows // tm
    tail_blocks = tail // tm
    first_tail_step = pad_blocks + nt - tail_blocks

    def row(bi, j):
        return (bi, jnp.maximum(j - pad_blocks, 0), 0)

    def padded(bi, j):
        return (bi, j, 0)

    def tail_map(bi, j):
        return (bi, jnp.maximum(j - first_tail_step, 0), 0)

    def const(bi, j):
        return (0, 0)

    def out(width, dtype, rows=t):
        return jax.ShapeDtypeStruct((b, rows, width), dtype)

    blk = lambda width, imap: pl.BlockSpec((1, tm, width), imap)
    out_shape = [
        out(WIDTH, BF16), out(WIDTH, F32), out(WIDTH, BF16, t + pad_rows),
        out(WIDTH, F32), out(WIDTH, BF16, t + pad_rows), out(WIDTH, BF16),
        out(WIDTH, BF16), out(WIDTH, BF16, t + pad_rows), out(WIDTH, BF16, t + pad_rows),
        out(WIDTH, F32, tail), out(WIDTH, F32, tail), out(WIDTH, BF16),
        out(D_MODEL, BF16), out(D_MODEL, BF16),
    ]
    out_specs = [
        blk(WIDTH, row), blk(WIDTH, row), blk(WIDTH, padded),
        blk(WIDTH, row), blk(WIDTH, padded), blk(WIDTH, row),
        blk(WIDTH, row), blk(WIDTH, padded), blk(WIDTH, padded),
        blk(WIDTH, tail_map), blk(WIDTH, tail_map), blk(WIDTH, row),
        blk(D_MODEL, row), blk(D_MODEL, row),
    ]
    in_specs = [
        blk(D_MODEL, row),
        pl.BlockSpec((1, D_MODEL), const),
        pl.BlockSpec((D_MODEL, IN_COLS), const, pipeline_mode=pl.Buffered(1)),
        pl.BlockSpec((1, LANES), const),
        pl.BlockSpec((1, LANES), const),
    ]
    return pl.pallas_call(
        functools.partial(_proj_kernel, pad_blocks=pad_blocks),
        grid=(b, nt + pad_blocks),
        in_specs=in_specs, out_specs=out_specs, out_shape=out_shape,
        compiler_params=pltpu.CompilerParams(
            dimension_semantics=("arbitrary", "arbitrary"), vmem_limit_bytes=VMEM_LIMIT),
        name="project",
    )(x, nw, w_bf, qnw128, knw128)


def _sb_kernel(*refs, blk, new_pad, n_cache):
    if n_cache:
        (q_ref, kn_ref, vn_ref, kc_ref, vc_ref, uo_ref, o_ref,
         qm_ref, acc_ref, carry_ref, z_ref, sc_ref) = refs
    else:
        (q_ref, kn_ref, vn_ref, uo_ref, o_ref,
         qm_ref, acc_ref, carry_ref, z_ref, sc_ref) = refs
    i = pl.program_id(1)
    lane_lo = lax.broadcasted_iota(jnp.int32, (blk, LANES), 1) < HEAD_DIM
    uo = uo_ref[...]

    head = lambda h: slice(h * HEAD_DIM, (h + 1) * HEAD_DIM)
    pair = lambda h: slice((h // 2) * LANES, (h // 2 + 1) * LANES)
    if new_pad is not None:
        for hp in range(HEAD_PAIRS):
            q2 = q_ref[0, :, pair(2 * hp)]
            zero = jnp.zeros_like(q2)
            qm_ref[2 * hp] = jnp.where(lane_lo, q2, zero)
            qm_ref[2 * hp + 1] = jnp.where(lane_lo, zero, q2)
    else:
        for h in range(N_HEADS):
            qm_ref[h] = q_ref[0, :, head(h)]
    acc_ref[...] = jnp.zeros(acc_ref.shape, F32)
    carry_ref[...] = jnp.zeros(carry_ref.shape, F32)

    def visit(scores, weighted, n, causal):
        sub = lambda u: slice(u * blk, (u + 1) * blk)
        masked = lambda u: causal is not None and u == n - 1
        for h in range(N_HEADS):
            z_ref[h, :, :n * blk] = scores(h, qm_ref[h])
        for h in range(N_HEADS):
            zh = z_ref[h, :, :n * blk]
            sp = jnp.maximum(zh, 0.0) + jnp.log(1.0 + jnp.exp2(jnp.abs(zh) * (-LOG2E)))
            for u in range(n):
                spu = jnp.where(causal, sp[:, sub(u)], 0.0) if masked(u) else sp[:, sub(u)]
                hi = spu.astype(BF16)
                lo = (spu - hi.astype(F32)).astype(BF16)
                sc_ref[h, u] = _dot(jnp.concatenate([hi, lo], axis=1), uo)
        for h in range(N_HEADS):
            carry = carry_ref[h]
            ws = [None] * n
            for u in reversed(range(n)):
                w = jnp.exp(z_ref[h, :, sub(u)] - sc_ref[h, u, :, :blk] - carry)
                if masked(u):
                    w = jnp.where(causal, w, 0.0)
                ws[u] = w.astype(BF16)
                carry = carry + sc_ref[h, u, :, blk:]
            carry_ref[h] = carry
            acc_ref[h] += weighted(h, jnp.concatenate(ws, axis=1))
        return jnp.min(carry_ref[...]) < -EXP_ZERO_BELOW

    row = lax.broadcasted_iota(jnp.int32, (blk, blk), 0)
    colm = lax.broadcasted_iota(jnp.int32, (blk, blk), 1)
    causal = colm < row
    diag = pl.multiple_of((new_pad or 0) + i * blk, blk)

    if new_pad is not None:
        def new_rows(s, n):
            k_tile = kn_ref[0, pl.ds(s, n * blk), :]
            v_tile = vn_ref[0, pl.ds(s, n * blk), :]
            return (lambda h, q: _dot_nt(q, k_tile[:, pair(h)]),
                    lambda h, w: _dot(w, v_tile[:, pair(h)]), n)

        first = pl.multiple_of(diag - 2 * blk, blk)
        live = visit(*new_rows(first, 3), causal)

        def older_new(c):
            j, _ = c
            s = pl.multiple_of(diag - (j + 1) * 2 * blk, blk)
            return j + 1, visit(*new_rows(s, 2), None)

        def cond_new(c):
            j, live = c
            return jnp.logical_and(j < (i + 1) // 2, live)
        lax.while_loop(cond_new, older_new, (jnp.int32(1), live))

        for hp in range(HEAD_PAIRS):
            o_ref[0, :, pair(2 * hp)] = jnp.where(
                lane_lo, acc_ref[2 * hp], acc_ref[2 * hp + 1]).astype(o_ref.dtype)
    else:
        def cache_cols(ref, h, s):
            return ref[0, h, :, pl.ds(s, 2 * blk)].astype(BF16)

        def cache_rows(s):
            return (lambda h, q: _dot(q, cache_cols(kc_ref, h, s)),
                    lambda h, w: _dot_nt(w, cache_cols(vc_ref, h, s)), 2)

        top = (n_cache - 1) * 2 * blk
        old_scores, old_weighted, _ = cache_rows(top)
        live = visit(
            lambda h, q: jnp.concatenate(
                [old_scores(h, q), _dot_nt(q, kn_ref[0, :, head(h)])], axis=1),
            lambda h, w: (old_weighted(h, w[:, :2 * blk])
                          + _dot(w[:, 2 * blk:], vn_ref[0, :, head(h)])),
            3, causal)

        def older_cache(c):
            j, _ = c
            s = pl.multiple_of(j * 2 * blk, 2 * blk)
            return j - 1, visit(*cache_rows(s), None)

        def cond_cache(c):
            j, live = c
            return jnp.logical_and(j >= 0, live)
        lax.while_loop(cond_cache, older_cache, (jnp.int32(n_cache - 2), live))

        for h in range(N_HEADS):
            o_ref[0, :, head(h)] = acc_ref[h].astype(o_ref.dtype)


def _cumsum_matrix(blk):
    j = np.arange(blk)[:, None]
    s = np.arange(blk)[None, :]
    u = (j >= s).astype(np.float32)
    half = np.concatenate([u, np.ones((blk, blk), np.float32)], axis=1)
    return jnp.asarray(np.concatenate([half, half], axis=0), dtype=BF16)


def _stick_breaking(q, k_new, v_new, k_cache=None, v_cache=None, *, blk):
    b, tq, _ = q.shape
    nq = tq // blk
    new_pad = k_new.shape[1] - tq if nq > 1 else None
    assert (new_pad is None) == (k_cache is not None)
    assert new_pad is None or new_pad >= 2 * blk
    n_cache = 0 if k_cache is None else k_cache.shape[-1] // (2 * blk)
    d = HEAD_DIM if n_cache else LANES
    full = lambda arr: pl.BlockSpec((1,) + arr.shape[1:],
                                    lambda bi, i: (bi,) + (0,) * (arr.ndim - 1))
    qblk = pl.BlockSpec((1, blk, WIDTH), lambda bi, i: (bi, i, 0))
    uo = _cumsum_matrix(blk)
    operands = [q, k_new, v_new]
    in_specs = [qblk, full(k_new), full(v_new)]
    if n_cache:
        operands += [k_cache, v_cache]
        in_specs += [full(k_cache), full(v_cache)]
    operands.append(uo)
    in_specs.append(pl.BlockSpec(uo.shape, lambda bi, i: (0, 0)))
    return pl.pallas_call(
        functools.partial(_sb_kernel, blk=blk, new_pad=new_pad, n_cache=n_cache),
        grid=(b, nq),
        in_specs=in_specs, out_specs=qblk,
        out_shape=jax.ShapeDtypeStruct((b, tq, WIDTH), BF16),
        scratch_shapes=[pltpu.VMEM((N_HEADS, blk, d), BF16),
                        pltpu.VMEM((N_HEADS, blk, d), F32),
                        pltpu.VMEM((N_HEADS, blk, blk), F32),
                        pltpu.VMEM((N_HEADS, blk, 3 * blk), F32),
                        pltpu.VMEM((N_HEADS, 3, blk, 2 * blk), F32)],
        compiler_params=pltpu.CompilerParams(
            dimension_semantics=("arbitrary", "arbitrary"), vmem_limit_bytes=VMEM_LIMIT),
        name="stick_breaking",
    )(*operands)


def _bias_kernel(rb_ref, o_ref, *, g):
    rows, win = g * CHUNK, (LEFT_CHUNKS + g) * CHUNK
    n = 768
    npad = rb_ref.shape[1]
    jj = lax.broadcasted_iota(jnp.int32, (npad, n), 1)
    rr = lax.broadcasted_iota(jnp.int32, (npad, n), 0)
    idx = jnp.where(jj > win, 2 * REL_CLIP,
                    jnp.clip(BAND - jj, -REL_CLIP, REL_CLIP) + REL_CLIP)
    sel = (rr == idx).astype(BF16)
    rb = rb_ref[...]
    hi = rb.astype(BF16)
    r1 = rb - hi.astype(F32)
    mid = r1.astype(BF16)
    lo = (r1 - mid.astype(F32)).astype(BF16)
    line = _dot(hi, sel) + _dot(mid, sel) + _dot(lo, sel)
    line = line * LOG2E
    tq = lax.broadcasted_iota(jnp.int32, (rows, win), 0)
    tk = lax.broadcasted_iota(jnp.int32, (rows, win), 1)
    off = tk - (tq // CHUNK) * CHUNK
    valid = jnp.logical_and(off >= 0, off < (LEFT_CHUNKS + 1) * CHUNK)
    for h in range(N_HEADS):
        tiled = jnp.broadcast_to(line[h:h + 1, :], (rows, n))
        rolled = pltpu.roll(tiled, 0, 1, stride=1, stride_axis=0)
        o_ref[h] = jnp.where(valid, rolled[:, :win], NEG_INF)


def _bias_table(rb_pad, g):
    rows, win = g * CHUNK, (LEFT_CHUNKS + g) * CHUNK
    return pl.pallas_call(
        functools.partial(_bias_kernel, g=g),
        out_shape=jax.ShapeDtypeStruct((N_HEADS, rows, win), F32),
        name="bias_table",
    )(rb_pad)


def _cb_kernel(q_ref, k_ref, v_ref, bias_ref, o_ref, *scratch, rows, win, pad_history):
    s_refs, m_refs = scratch[:HEAD_PAIRS], scratch[HEAD_PAIRS:]
    p = pl.program_id(1)
    start = pl.multiple_of(p * rows, rows)
    lane_lo = lax.broadcasted_iota(jnp.int32, (rows, LANES), 1) < HEAD_DIM
    zero_idx = jnp.minimum(p, 0)

    def body(mask_pad):
        if mask_pad:
            key_ok = lax.broadcasted_iota(jnp.int32, (2 * rows, win), 1) >= BAND - p * rows
        for hp in range(HEAD_PAIRS):
            sl = slice(hp * LANES, (hp + 1) * LANES)
            q2 = q_ref[0, :, sl]
            zero = jnp.zeros_like(q2)
            qm = jnp.concatenate([jnp.where(lane_lo, q2, zero), jnp.where(lane_lo, zero, q2)],
                                 axis=0)
            s = _dot_nt(qm, k_ref[0, pl.ds(start, win), sl]) + bias_ref[hp]
            if mask_pad:
                s = jnp.where(key_ok, s, NEG_INF)
            s_refs[hp][0] = s
            m_refs[hp][0] = jnp.broadcast_to(jnp.max(s, axis=-1, keepdims=True),
                                             (2 * rows, LANES))
        for hp in range(HEAD_PAIRS):
            sl = slice(hp * LANES, (hp + 1) * LANES)
            s = s_refs[hp][zero_idx]
            if win % LANES == 0:
                m = jnp.concatenate([m_refs[hp][zero_idx]] * (win // LANES), axis=1)
            else:
                m = m_refs[hp][zero_idx][:, :1]
            e = jnp.exp2(s - m)
            l = jnp.sum(e, axis=-1, keepdims=True)
            o = _dot(e.astype(BF16), v_ref[0, pl.ds(start, win), sl]) / l
            o_ref[0, :, sl] = jnp.where(lane_lo, o[:rows], o[rows:]).astype(o_ref.dtype)

    if pad_history:
        pad_tiles = BAND // rows
        pl.when(p < pad_tiles)(functools.partial(body, True))
        pl.when(p >= pad_tiles)(functools.partial(body, False))
    else:
        body(False)


def _chunk_band(q, k_hist, v_hist, bias, *, g, pad_history):
    b, t, _ = q.shape
    rows, win = g * CHUNK, (LEFT_CHUNKS + g) * CHUNK
    full = lambda arr: pl.BlockSpec((1,) + arr.shape[1:], lambda bi, i: (bi, 0, 0))
    qblk = pl.BlockSpec((1, rows, WIDTH), lambda bi, i: (bi, i, 0))
    bias = bias.reshape(HEAD_PAIRS, 2 * rows, win)
    return pl.pallas_call(
        functools.partial(_cb_kernel, rows=rows, win=win, pad_history=pad_history),
        grid=(b, t // rows),
        in_specs=[qblk, full(k_hist), full(v_hist),
                  pl.BlockSpec(bias.shape, lambda bi, i: (0, 0, 0))],
        out_specs=qblk,
        out_shape=jax.ShapeDtypeStruct((b, t, WIDTH), BF16),
        scratch_shapes=([pltpu.VMEM((1, 2 * rows, win), F32)] * HEAD_PAIRS
                        + [pltpu.VMEM((1, 2 * rows, LANES), F32)] * HEAD_PAIRS),
        compiler_params=pltpu.CompilerParams(
            dimension_semantics=("arbitrary", "arbitrary"), vmem_limit_bytes=VMEM_LIMIT),
        name="chunk_band",
    )(q, k_hist, v_hist, bias)


def _merge_kernel(x_ref, osb_ref, zsb_ref, ocb_ref, zcb_ref, gsb_ref, gcb_ref,
                  wsb_ref, wcb_ref, wout_ref, y_ref):
    zsb = zsb_ref[0].astype(F32)
    zcb = zcb_ref[0].astype(F32)
    a_sb = (osb_ref[0].astype(F32) * (zsb * _sigmoid(zsb))).astype(BF16)
    a_cb = (ocb_ref[0].astype(F32) * (zcb * _sigmoid(zcb))).astype(BF16)
    b_sb = _dot(a_sb, wsb_ref[...])
    b_cb = _dot(a_cb, wcb_ref[...])
    h = (_sigmoid(gsb_ref[0].astype(F32)) * b_sb
         + _sigmoid(gcb_ref[0].astype(F32)) * b_cb)
    y_ref[0] = x_ref[0] + _dot(h.astype(BF16), wout_ref[...])


def _merge(x, osb, zsb, ocb, zcb, gsb, gcb, wsb, wcb, wout, *, tm):
    b, t, _ = x.shape
    wide = pl.BlockSpec((1, tm, D_MODEL), lambda bi, i: (bi, i, 0))
    narrow = pl.BlockSpec((1, tm, WIDTH), lambda bi, i: (bi, i, 0))
    const = lambda arr: pl.BlockSpec(arr.shape, lambda bi, i: (0, 0))
    return pl.pallas_call(
        _merge_kernel,
        grid=(b, t // tm),
        in_specs=[wide, narrow, narrow, narrow, narrow, wide, wide,
                  const(wsb), const(wcb), const(wout)],
        out_specs=wide,
        out_shape=jax.ShapeDtypeStruct((b, t, D_MODEL), F32),
        compiler_params=pltpu.CompilerParams(
            dimension_semantics=("arbitrary", "arbitrary"), vmem_limit_bytes=VMEM_LIMIT),
        name="merge",
    )(x, osb, zsb, ocb, zcb, gsb, gcb, wsb, wcb, wout)


def _heads(a):
    b, t, _ = a.shape
    return a.reshape(b, t, N_HEADS, HEAD_DIM)


def kernel(x_prompt, x_sample, cache_sb_k, cache_sb_v, cache_cb_k, cache_cb_v, norm_w, w_in,
           q_norm_w, k_norm_w, rel_bias, w_proj_sb, w_proj_cb, w_out):
    depth = w_in.shape[0]
    b, t, _ = x_prompt.shape
    bs, tn, _ = x_sample.shape
    p = cache_sb_k.shape[2]
    r = cache_cb_k.shape[2]
    assert t % (2 * CHUNK) == 0 and tn == CHUNK and r == BAND and p % (2 * tn) == 0

    y_p, y_s = x_prompt, x_sample
    outs = [[] for _ in range(8)]
    for l in range(depth):
        nw = norm_w[l].reshape(1, D_MODEL)
        w_bf = w_in[l].astype(BF16)
        qnw = jnp.tile(q_norm_w[l], 2).reshape(1, LANES)
        knw = jnp.tile(k_norm_w[l], 2).reshape(1, LANES)
        rb_pad = jnp.pad(rel_bias[l], ((0, 0), (0, 384 - (2 * REL_CLIP + 1))))
        bias2 = _bias_table(rb_pad, 2)
        bias1 = bias2[:, :CHUNK, :(LEFT_CHUNKS + 1) * CHUNK]
        wsb = w_proj_sb[l].astype(BF16)
        wcb = w_proj_cb[l].astype(BF16)
        wout = w_out[l].astype(BF16)

        (sbq, sbk32, sbk16, sbv32, sbv16, sbz, cbq, cbk16, cbv16, cbk32, cbv32, cbz,
         gsb, gcb) = _project(y_p, nw, w_bf, qnw, knw, tm=512, pad_rows=BAND, tail=min(BAND, t))
        o_sb = _stick_breaking(sbq, sbk16, sbv16, blk=128)
        o_cb = _chunk_band(cbq, cbk16, cbv16, bias2, g=2, pad_history=True)
        y_p_next = _merge(y_p, o_sb, sbz, o_cb, cbz, gsb, gcb, wsb, wcb, wout, tm=512)
        for lst, a in zip(outs[:4], (sbk32, sbv32, cbk32, cbv32)):
            lst.append(_heads(a))

        proj = _project(y_s.reshape(1, bs * tn, D_MODEL), nw, w_bf, qnw, knw,
                        tm=bs * tn, pad_rows=0, tail=bs * tn)
        (sbq, sbk32, sbk16, sbv32, sbv16, sbz, cbq, cbk16, cbv16, cbk32, cbv32, cbz,
         gsb, gcb) = [a.reshape(bs, tn, a.shape[-1]) for a in proj]
        o_sb = _stick_breaking(sbq, sbk16, sbv16,
                               jnp.transpose(cache_sb_k[l], (0, 2, 3, 1)),
                               jnp.transpose(cache_sb_v[l], (0, 2, 3, 1)), blk=tn)
        k_hist = jnp.concatenate([cache_cb_k[l].reshape(bs, r, WIDTH).astype(BF16), cbk16], axis=1)
        v_hist = jnp.concatenate([cache_cb_v[l].reshape(bs, r, WIDTH).astype(BF16), cbv16], axis=1)
        o_cb = _chunk_band(cbq, k_hist, v_hist, bias1, g=1, pad_history=False)
        flat = lambda a: a.reshape(1, bs * tn, a.shape[-1])
        y_s_next = _merge(flat(y_s), flat(o_sb), flat(sbz), flat(o_cb), flat(cbz), flat(gsb),
                          flat(gcb), wsb, wcb, wout, tm=bs * tn).reshape(bs, tn, D_MODEL)
        for lst, a in zip(outs[4:], (sbk32, sbv32, cbk32, cbv32)):
            lst.append(_heads(a))

        y_p, y_s = y_p_next, y_s_next

    return (y_p, y_s) + tuple(jnp.stack(o) for o in outs)
```

```python
import functools

import jax
import jax.numpy as jnp
import numpy as np
from jax import lax
from jax.experimental import pallas as pl
from jax.experimental.pallas import tpu as pltpu

F32 = jnp.float32
BF16 = jnp.bfloat16

D_MODEL = 1024
CHUNK = 64
LEFT_CHUNKS = 8
BAND = LEFT_CHUNKS * CHUNK
N_HEADS = 8
HEAD_DIM = 64
WIDTH = N_HEADS * HEAD_DIM
REL_CLIP = 128
EPS = 1e-6
NEG_INF = -1e30
ATTN_SCALE = 0.125
IN_COLS = 4 * WIDTH + 4 * WIDTH + 2 * D_MODEL

LANES = 128
HEAD_PAIRS = N_HEADS // 2
EXP_ZERO_BELOW = -104.0
LOG2E = 1.4426950408889634
VMEM_LIMIT = 56 * 1024 * 1024


def _dot(a, b):
    return jnp.dot(a, b, preferred_element_type=F32)


def _dot_nt(a, b):
    return lax.dot_general(a, b, (((1,), (1,)), ((), ())), preferred_element_type=F32)


def _sigmoid(x):
    return 1.0 / (1.0 + jnp.exp(-x))


def _head_rms_norm(h, w128, lane_lo):
    sq = h * h
    s_lo = jnp.sum(jnp.where(lane_lo, sq, 0.0), axis=-1, keepdims=True)
    s_hi = jnp.sum(jnp.where(lane_lo, 0.0, sq), axis=-1, keepdims=True)
    r_lo = lax.rsqrt(s_lo * (1.0 / HEAD_DIM) + EPS)
    r_hi = lax.rsqrt(s_hi * (1.0 / HEAD_DIM) + EPS)
    return (h * jnp.where(lane_lo, r_lo, r_hi)) * w128


def _proj_kernel(x_ref, nw_ref, w_ref, qnw_ref, knw_ref,
                 sbq_ref, sbk32_ref, sbk16_ref, sbv32_ref, sbv16_ref, sbz_ref,
                 cbq_ref, cbk16_ref, cbv16_ref, cbk32_ref, cbv32_ref, cbz_ref,
                 gsb_ref, gcb_ref, *, pad_blocks):
    j = pl.program_id(1)

    if pad_blocks:
        @pl.when(j < pad_blocks)
        def _():
            for ref in (sbk16_ref, sbv16_ref, cbk16_ref, cbv16_ref):
                ref[...] = jnp.zeros(ref.shape, BF16)

    @pl.when(j >= pad_blocks)
    def _():
        x = x_ref[0]
        ms = jnp.mean(x * x, axis=-1, keepdims=True)
        xb = ((x * lax.rsqrt(ms + EPS)) * nw_ref[...]).astype(BF16)
        tm = x.shape[0]
        lane_lo = lax.broadcasted_iota(jnp.int32, (tm, LANES), 1) < HEAD_DIM

        def col(c0, width):
            return _dot(xb, w_ref[:, c0:c0 + width])

        sbq_ref[0] = (col(0, WIDTH) * ATTN_SCALE).astype(BF16)
        h = col(WIDTH, WIDTH)
        sbk32_ref[0] = h
        sbk16_ref[0] = h.astype(BF16)
        h = col(2 * WIDTH, WIDTH)
        sbv32_ref[0] = h
        sbv16_ref[0] = h.astype(BF16)
        sbz_ref[0] = col(3 * WIDTH, WIDTH).astype(BF16)

        hq_all = col(4 * WIDTH, WIDTH)
        hk_all = col(5 * WIDTH, WIDTH)
        for s in range(HEAD_PAIRS):
            sl = slice(s * LANES, (s + 1) * LANES)
            hq = _head_rms_norm(hq_all[:, sl], qnw_ref[...], lane_lo)
            cbq_ref[0, :, sl] = (hq * (ATTN_SCALE * LOG2E)).astype(BF16)
            hk = _head_rms_norm(hk_all[:, sl], knw_ref[...], lane_lo)
            cbk16_ref[0, :, sl] = hk.astype(BF16)
            cbk32_ref[0, :, sl] = hk

        h = col(6 * WIDTH, WIDTH)
        cbv16_ref[0] = h.astype(BF16)
        cbv32_ref[0] = h

        cbz_ref[0] = col(7 * WIDTH, WIDTH).astype(BF16)
        gsb_ref[0] = col(8 * WIDTH, D_MODEL).astype(BF16)
        gcb_ref[0] = col(8 * WIDTH + D_MODEL, D_MODEL).astype(BF16)


def _project(x, nw, w_bf, qnw128, knw128, *, tm, pad_rows, tail):
    b, t, _ = x.shape
    nt = t // tm
    pad_blocks = pad_rows // tm
    tail_blocks = tail // tm
    first_tail_step = pad_blocks + nt - tail_blocks

    def row(bi, j):
        return (bi, jnp.maximum(j - pad_blocks, 0), 0)

    def padded(bi, j):
        return (bi, j, 0)

    def tail_map(bi, j):
        return (bi, jnp.maximum(j - first_tail_step, 0), 0)

    def const(bi, j):
        return (0, 0)

    def out(width, dtype, rows=t):
        return jax.ShapeDtypeStruct((b, rows, width), dtype)

    blk = lambda width, imap: pl.BlockSpec((1, tm, width), imap)
    out_shape = [
        out(WIDTH, BF16), out(WIDTH, F32), out(WIDTH, BF16, t + pad_rows),
        out(WIDTH, F32), out(WIDTH, BF16, t + pad_rows), out(WIDTH, BF16),
        out(WIDTH, BF16), out(WIDTH, BF16, t + pad_rows), out(WIDTH, BF16, t + pad_rows),
        out(WIDTH, F32, tail), out(WIDTH, F32, tail), out(WIDTH, BF16),
        out(D_MODEL, BF16), out(D_MODEL, BF16),
    ]
    out_specs = [
        blk(WIDTH, row), blk(WIDTH, row), blk(WIDTH, padded),
        blk(WIDTH, row), blk(WIDTH, padded), blk(WIDTH, row),
        blk(WIDTH, row), blk(WIDTH, padded), blk(WIDTH, padded),
        blk(WIDTH, tail_map), blk(WIDTH, tail_map), blk(WIDTH, row),
        blk(D_MODEL, row), blk(D_MODEL, row),
    ]
    in_specs = [
        blk(D_MODEL, row),
        pl.BlockSpec((1, D_MODEL), const),
        pl.BlockSpec((D_MODEL, IN_COLS), const, pipeline_mode=pl.Buffered(1)),
        pl.BlockSpec((1, LANES), const),
        pl.BlockSpec((1, LANES), const),
    ]
    return pl.pallas_call(
        functools.partial(_proj_kernel, pad_blocks=pad_blocks),
        grid=(b, nt + pad_blocks),
        in_specs=in_specs, out_specs=out_specs, out_shape=out_shape,
        compiler_params=pltpu.CompilerParams(
            dimension_semantics=("arbitrary", "arbitrary"), vmem_limit_bytes=VMEM_LIMIT),
        name="project",
    )(x, nw, w_bf, qnw128, knw128)


def _sb_kernel(*refs, blk, qtiles, new_pad, n_cache):
    if n_cache:
        (q_ref, kn_ref, vn_ref, kc_ref, vc_ref, uo_ref, o_ref,
         qm_all, acc_all, carry_all, z_all, sc_all) = refs
    else:
        (q_ref, kn_ref, vn_ref, uo_ref, o_ref,
         qm_all, acc_all, carry_all, z_all, sc_all) = refs
    lane_lo = lax.broadcasted_iota(jnp.int32, (blk, LANES), 1) < HEAD_DIM
    uo = uo_ref[...]
    row = lax.broadcasted_iota(jnp.int32, (blk, blk), 0)
    colm = lax.broadcasted_iota(jnp.int32, (blk, blk), 1)
    causal = colm < row
    head = lambda h: slice(h * HEAD_DIM, (h + 1) * HEAD_DIM)
    pair = lambda h: slice((h // 2) * LANES, (h // 2 + 1) * LANES)

    def visit(t, scores, weighted, n, causal):
        qm_ref, acc_ref, carry_ref = qm_all.at[t], acc_all.at[t], carry_all.at[t]
        z_ref, sc_ref = z_all.at[t], sc_all.at[t]
        sub = lambda u: slice(u * blk, (u + 1) * blk)
        masked = lambda u: causal is not None and u == n - 1
        for h in range(N_HEADS):
            z_ref[h, :, :n * blk] = scores(h, qm_ref[h])
        for h in range(N_HEADS):
            zh = z_ref[h, :, :n * blk]
            sp = jnp.maximum(zh, 0.0) + jnp.log(1.0 + jnp.exp2(jnp.abs(zh) * (-LOG2E)))
            for u in range(n):
                spu = jnp.where(causal, sp[:, sub(u)], 0.0) if masked(u) else sp[:, sub(u)]
                hi = spu.astype(BF16)
                lo = (spu - hi.astype(F32)).astype(BF16)
                sc_ref[h, u] = _dot(jnp.concatenate([hi, lo], axis=1), uo)
        for h in range(N_HEADS):
            carry = carry_ref[h]
            ws = [None] * n
            for u in reversed(range(n)):
                w = jnp.exp(z_ref[h, :, sub(u)] - sc_ref[h, u, :, :blk] - carry)
                if masked(u):
                    w = jnp.where(causal, w, 0.0)
                ws[u] = w.astype(BF16)
                carry = carry + sc_ref[h, u, :, blk:]
            carry_ref[h] = carry
            acc_ref[h] += weighted(h, jnp.concatenate(ws, axis=1))
        return jnp.min(carry_ref[...]) < -EXP_ZERO_BELOW

    acc_all[...] = jnp.zeros(acc_all.shape, F32)
    carry_all[...] = jnp.zeros(carry_all.shape, F32)

    if new_pad is not None:
        blocks = [pl.program_id(1) * qtiles + t for t in range(qtiles)]
        diags = [pl.multiple_of(new_pad + ib * blk, blk) for ib in blocks]
        rows_of = lambda t: slice(t * blk, (t + 1) * blk)
        for t in range(qtiles):
            for hp in range(HEAD_PAIRS):
                q2 = q_ref[0, rows_of(t), pair(2 * hp)]
                zero = jnp.zeros_like(q2)
                qm_all[t, 2 * hp] = jnp.where(lane_lo, q2, zero)
                qm_all[t, 2 * hp + 1] = jnp.where(lane_lo, zero, q2)

        def new_rows(s, n):
            k_tile = kn_ref[0, pl.ds(s, n * blk), :]
            v_tile = vn_ref[0, pl.ds(s, n * blk), :]
            return (lambda h, q: _dot_nt(q, k_tile[:, pair(h)]),
                    lambda h, w: _dot(w, v_tile[:, pair(h)]), n)

        lives = [visit(t, *new_rows(pl.multiple_of(diags[t] - 2 * blk, blk), 3), causal)
                 for t in range(qtiles)]

        for t in range(qtiles):
            def older_new(c, t=t):
                j, _ = c
                s = pl.multiple_of(diags[t] - (j + 1) * 2 * blk, blk)
                return j + 1, visit(t, *new_rows(s, 2), None)

            def cond_new(c, t=t):
                j, live = c
                return jnp.logical_and(j < (blocks[t] + 1) // 2, live)
            lax.while_loop(cond_new, older_new, (jnp.int32(1), lives[t]))

        for t in range(qtiles):
            for hp in range(HEAD_PAIRS):
                o_ref[0, rows_of(t), pair(2 * hp)] = jnp.where(
                    lane_lo, acc_all[t, 2 * hp], acc_all[t, 2 * hp + 1]).astype(o_ref.dtype)
    else:
        for h in range(N_HEADS):
            qm_all[0, h] = q_ref[0, :, head(h)]

        def cache_cols(ref, h, s):
            return ref[0, h, :, pl.ds(s, 2 * blk)].astype(BF16)

        def cache_rows(s):
            return (lambda h, q: _dot(q, cache_cols(kc_ref, h, s)),
                    lambda h, w: _dot_nt(w, cache_cols(vc_ref, h, s)), 2)

        top = (n_cache - 1) * 2 * blk
        old_scores, old_weighted, _ = cache_rows(top)
        live = visit(
            0,
            lambda h, q: jnp.concatenate(
                [old_scores(h, q), _dot_nt(q, kn_ref[0, :, head(h)])], axis=1),
            lambda h, w: (old_weighted(h, w[:, :2 * blk])
                          + _dot(w[:, 2 * blk:], vn_ref[0, :, head(h)])),
            3, causal)

        def older_cache(c):
            j, _ = c
            s = pl.multiple_of(j * 2 * blk, 2 * blk)
            return j - 1, visit(0, *cache_rows(s), None)

        def cond_cache(c):
            j, live = c
            return jnp.logical_and(j >= 0, live)
        lax.while_loop(cond_cache, older_cache, (jnp.int32(n_cache - 2), live))

        for h in range(N_HEADS):
            o_ref[0, :, head(h)] = acc_all[0, h].astype(o_ref.dtype)


def _cumsum_matrix(blk):
    j = np.arange(blk)[:, None]
    s = np.arange(blk)[None, :]
    u = (j >= s).astype(np.float32)
    half = np.concatenate([u, np.ones((blk, blk), np.float32)], axis=1)
    return jnp.asarray(np.concatenate([half, half], axis=0), dtype=BF16)


def _stick_breaking(q, k_new, v_new, k_cache=None, v_cache=None, *, blk):
    b, tq, _ = q.shape
    nq = tq // blk
    new_pad = k_new.shape[1] - tq if nq > 1 else None
    assert (new_pad is None) == (k_cache is not None)
    assert new_pad is None or new_pad >= 2 * blk
    n_cache = 0 if k_cache is None else k_cache.shape[-1] // (2 * blk)
    d = HEAD_DIM if n_cache else LANES
    full = lambda arr: pl.BlockSpec((1,) + arr.shape[1:],
                                    lambda bi, i: (bi,) + (0,) * (arr.ndim - 1))
    qtiles = 2 if nq % 2 == 0 else 1
    qblk = pl.BlockSpec((1, qtiles * blk, WIDTH), lambda bi, i: (bi, i, 0))
    uo = _cumsum_matrix(blk)
    operands = [q, k_new, v_new]
    in_specs = [qblk, full(k_new), full(v_new)]
    if n_cache:
        operands += [k_cache, v_cache]
        in_specs += [full(k_cache), full(v_cache)]
    operands.append(uo)
    in_specs.append(pl.BlockSpec(uo.shape, lambda bi, i: (0, 0)))
    return pl.pallas_call(
        functools.partial(_sb_kernel, blk=blk, qtiles=qtiles, new_pad=new_pad, n_cache=n_cache),
        grid=(b, nq // qtiles),
        in_specs=in_specs, out_specs=qblk,
        out_shape=jax.ShapeDtypeStruct((b, tq, WIDTH), BF16),
        scratch_shapes=[pltpu.VMEM((qtiles, N_HEADS, blk, d), BF16),
                        pltpu.VMEM((qtiles, N_HEADS, blk, d), F32),
                        pltpu.VMEM((qtiles, N_HEADS, blk, blk), F32),
                        pltpu.VMEM((qtiles, N_HEADS, blk, 3 * blk), F32),
                        pltpu.VMEM((qtiles, N_HEADS, 3, blk, 2 * blk), F32)],
        compiler_params=pltpu.CompilerParams(
            dimension_semantics=("arbitrary", "arbitrary"), vmem_limit_bytes=VMEM_LIMIT),
        name="stick_breaking",
    )(*operands)


def _bias_kernel(rb_ref, o_ref, *, g):
    rows, win = g * CHUNK, (LEFT_CHUNKS + g) * CHUNK
    n = 768
    npad = rb_ref.shape[1]
    jj = lax.broadcasted_iota(jnp.int32, (npad, n), 1)
    rr = lax.broadcasted_iota(jnp.int32, (npad, n), 0)
    idx = jnp.where(jj > win, 2 * REL_CLIP,
                    jnp.clip(BAND - jj, -REL_CLIP, REL_CLIP) + REL_CLIP)
    sel = (rr == idx).astype(BF16)
    rb = rb_ref[...]
    hi = rb.astype(BF16)
    r1 = rb - hi.astype(F32)
    mid = r1.astype(BF16)
    lo = (r1 - mid.astype(F32)).astype(BF16)
    line = _dot(hi, sel) + _dot(mid, sel) + _dot(lo, sel)
    line = line * LOG2E
    tq = lax.broadcasted_iota(jnp.int32, (rows, win), 0)
    tk = lax.broadcasted_iota(jnp.int32, (rows, win), 1)
    off = tk - (tq // CHUNK) * CHUNK
    valid = jnp.logical_and(off >= 0, off < (LEFT_CHUNKS + 1) * CHUNK)
    for h in range(N_HEADS):
        tiled = jnp.broadcast_to(line[h:h + 1, :], (rows, n))
        rolled = pltpu.roll(tiled, 0, 1, stride=1, stride_axis=0)
        o_ref[h] = jnp.where(valid, rolled[:, :win], NEG_INF)


def _bias_table(rb_pad, g):
    rows, win = g * CHUNK, (LEFT_CHUNKS + g) * CHUNK
    return pl.pallas_call(
        functools.partial(_bias_kernel, g=g),
        out_shape=jax.ShapeDtypeStruct((N_HEADS, rows, win), F32),
        name="bias_table",
    )(rb_pad)


def _cb_kernel(q_ref, k_ref, v_ref, bias_ref, o_ref, *scratch, rows, win, tiles, pad_history):
    n_slots = tiles * HEAD_PAIRS
    s_refs, m_refs = scratch[:n_slots], scratch[n_slots:]
    p = pl.program_id(1)
    lane_lo = lax.broadcasted_iota(jnp.int32, (rows, LANES), 1) < HEAD_DIM
    zero_idx = jnp.minimum(p, 0)
    work = [(t, hp) for t in range(tiles) for hp in range(HEAD_PAIRS)]

    def body(mask_pad):
        starts = [pl.multiple_of((p * tiles + t) * rows, rows) for t in range(tiles)]
        for t, hp in work:
            sl = slice(hp * LANES, (hp + 1) * LANES)
            q2 = q_ref[0, t * rows:(t + 1) * rows, sl]
            zero = jnp.zeros_like(q2)
            qm = jnp.concatenate([jnp.where(lane_lo, q2, zero), jnp.where(lane_lo, zero, q2)],
                                 axis=0)
            s = _dot_nt(qm, k_ref[0, pl.ds(starts[t], win), sl]) + bias_ref[hp]
            if mask_pad:
                key_ok = (lax.broadcasted_iota(jnp.int32, (2 * rows, win), 1)
                          >= BAND - starts[t])
                s = jnp.where(key_ok, s, NEG_INF)
            s_refs[t * HEAD_PAIRS + hp][0] = s
            m_refs[t * HEAD_PAIRS + hp][0] = jnp.broadcast_to(
                jnp.max(s, axis=-1, keepdims=True), (2 * rows, LANES))
        for t, hp in work:
            sl = slice(hp * LANES, (hp + 1) * LANES)
            s = s_refs[t * HEAD_PAIRS + hp][zero_idx]
            m = m_refs[t * HEAD_PAIRS + hp][zero_idx]
            m = jnp.concatenate([m] * (win // LANES), axis=1) if win % LANES == 0 else m[:, :1]
            e = jnp.exp2(s - m)
            l = jnp.sum(e, axis=-1, keepdims=True)
            o = _dot(e.astype(BF16), v_ref[0, pl.ds(starts[t], win), sl]) / l
            o_ref[0, t * rows:(t + 1) * rows, sl] = jnp.where(
                lane_lo, o[:rows], o[rows:]).astype(o_ref.dtype)

    if pad_history:
        pad_steps = BAND // (rows * tiles)
        pl.when(p < pad_steps)(functools.partial(body, True))
        pl.when(p >= pad_steps)(functools.partial(body, False))
    else:
        body(False)


def _chunk_band(q, k_hist, v_hist, bias, *, g, tiles, pad_history):
    b, t, _ = q.shape
    rows, win = g * CHUNK, (LEFT_CHUNKS + g) * CHUNK
    assert BAND % (rows * tiles) == 0 and t % (rows * tiles) == 0
    full = lambda arr: pl.BlockSpec((1,) + arr.shape[1:], lambda bi, i: (bi, 0, 0))
    qblk = pl.BlockSpec((1, tiles * rows, WIDTH), lambda bi, i: (bi, i, 0))
    bias = bias.reshape(HEAD_PAIRS, 2 * rows, win)
    return pl.pallas_call(
        functools.partial(_cb_kernel, rows=rows, win=win, tiles=tiles, pad_history=pad_history),
        grid=(b, t // (rows * tiles)),
        in_specs=[qblk, full(k_hist), full(v_hist),
                  pl.BlockSpec(bias.shape, lambda bi, i: (0, 0, 0))],
        out_specs=qblk,
        out_shape=jax.ShapeDtypeStruct((b, t, WIDTH), BF16),
        scratch_shapes=([pltpu.VMEM((1, 2 * rows, win), F32)] * (tiles * HEAD_PAIRS)
                        + [pltpu.VMEM((1, 2 * rows, LANES), F32)] * (tiles * HEAD_PAIRS)),
        compiler_params=pltpu.CompilerParams(
            dimension_semantics=("arbitrary", "arbitrary"), vmem_limit_bytes=VMEM_LIMIT),
        name="chunk_band",
    )(q, k_hist, v_hist, bias)


def _merge_kernel(x_ref, osb_ref, zsb_ref, ocb_ref, zcb_ref, gsb_ref, gcb_ref,
                  wsb_ref, wcb_ref, wout_ref, y_ref):
    zsb = zsb_ref[0].astype(F32)
    zcb = zcb_ref[0].astype(F32)
    a_sb = (osb_ref[0].astype(F32) * (zsb * _sigmoid(zsb))).astype(BF16)
    a_cb = (ocb_ref[0].astype(F32) * (zcb * _sigmoid(zcb))).astype(BF16)
    b_sb = _dot(a_sb, wsb_ref[...])
    b_cb = _dot(a_cb, wcb_ref[...])
    h = (_sigmoid(gsb_ref[0].astype(F32)) * b_sb
         + _sigmoid(gcb_ref[0].astype(F32)) * b_cb)
    y_ref[0] = x_ref[0] + _dot(h.astype(BF16), wout_ref[...])


def _merge(x, osb, zsb, ocb, zcb, gsb, gcb, wsb, wcb, wout, *, tm):
    b, t, _ = x.shape
    wide = pl.BlockSpec((1, tm, D_MODEL), lambda bi, i: (bi, i, 0))
    narrow = pl.BlockSpec((1, tm, WIDTH), lambda bi, i: (bi, i, 0))
    const = lambda arr: pl.BlockSpec(arr.shape, lambda bi, i: (0, 0))
    return pl.pallas_call(
        _merge_kernel,
        grid=(b, t // tm),
        in_specs=[wide, narrow, narrow, narrow, narrow, wide, wide,
                  const(wsb), const(wcb), const(wout)],
        out_specs=wide,
        out_shape=jax.ShapeDtypeStruct((b, t, D_MODEL), F32),
        compiler_params=pltpu.CompilerParams(
            dimension_semantics=("arbitrary", "arbitrary"), vmem_limit_bytes=VMEM_LIMIT),
        name="merge",
    )(x, osb, zsb, ocb, zcb, gsb, gcb, wsb, wcb, wout)


def _heads(a):
    b, t, _ = a.shape
    return a.reshape(b, t, N_HEADS, HEAD_DIM)


def kernel(x_prompt, x_sample, cache_sb_k, cache_sb_v, cache_cb_k, cache_cb_v, norm_w, w_in,
           q_norm_w, k_norm_w, rel_bias, w_proj_sb, w_proj_cb, w_out):
    depth = w_in.shape[0]
    b, t, _ = x_prompt.shape
    bs, tn, _ = x_sample.shape
    p = cache_sb_k.shape[2]
    r = cache_cb_k.shape[2]
    assert t % (2 * CHUNK) == 0 and tn == CHUNK and r == BAND and p % (2 * tn) == 0

    y_p, y_s = x_prompt, x_sample
    outs = [[] for _ in range(8)]
    for l in range(depth):
        nw = norm_w[l].reshape(1, D_MODEL)
        w_bf = w_in[l].astype(BF16)
        qnw = jnp.tile(q_norm_w[l], 2).reshape(1, LANES)
        knw = jnp.tile(k_norm_w[l], 2).reshape(1, LANES)
        rb_pad = jnp.pad(rel_bias[l], ((0, 0), (0, 384 - (2 * REL_CLIP + 1))))
        bias2 = _bias_table(rb_pad, 2)
        bias1 = bias2[:, :CHUNK, :(LEFT_CHUNKS + 1) * CHUNK]
        wsb = w_proj_sb[l].astype(BF16)
        wcb = w_proj_cb[l].astype(BF16)
        wout = w_out[l].astype(BF16)

        (sbq, sbk32, sbk16, sbv32, sbv16, sbz, cbq, cbk16, cbv16, cbk32, cbv32, cbz,
         gsb, gcb) = _project(y_p, nw, w_bf, qnw, knw, tm=512, pad_rows=BAND, tail=min(BAND, t))
        o_sb = _stick_breaking(sbq, sbk16, sbv16, blk=128)
        o_cb = _chunk_band(cbq, cbk16, cbv16, bias2, g=2, tiles=2, pad_history=True)
        y_p_next = _merge(y_p, o_sb, sbz, o_cb, cbz, gsb, gcb, wsb, wcb, wout, tm=512)
        for lst, a in zip(outs[:4], (sbk32, sbv32, cbk32, cbv32)):
            lst.append(_heads(a))

        proj = _project(y_s.reshape(1, bs * tn, D_MODEL), nw, w_bf, qnw, knw,
                        tm=bs * tn, pad_rows=0, tail=bs * tn)
        (sbq, sbk32, sbk16, sbv32, sbv16, sbz, cbq, cbk16, cbv16, cbk32, cbv32, cbz,
         gsb, gcb) = [a.reshape(bs, tn, a.shape[-1]) for a in proj]
        o_sb = _stick_breaking(sbq, sbk16, sbv16,
                               jnp.transpose(cache_sb_k[l], (0, 2, 3, 1)),
                               jnp.transpose(cache_sb_v[l], (0, 2, 3, 1)), blk=tn)
        k_hist = jnp.concatenate([cache_cb_k[l].reshape(bs, r, WIDTH).astype(BF16), cbk16], axis=1)
        v_hist = jnp.concatenate([cache_cb_v[l].reshape(bs, r, WIDTH).astype(BF16), cbv16], axis=1)
        o_cb = _chunk_band(cbq, k_hist, v_hist, bias1, g=1, tiles=1, pad_history=False)
        flat = lambda a: a.reshape(1, bs * tn, a.shape[-1])
        y_s_next = _merge(flat(y_s), flat(o_sb), flat(sbz), flat(o_cb), flat(cbz), flat(gsb),
                          flat(gcb), wsb, wcb, wout, tm=bs * tn).reshape(bs, tn, D_MODEL)
        for lst, a in zip(outs[4:], (sbk32, sbv32, cbk32, cbv32)):
            lst.append(_heads(a))

        y_p, y_s = y_p_next, y_s_next

    return (y_p, y_s) + tuple(jnp.stack(o) for o in outs)
```

```python
import functools

import jax
import jax.numpy as jnp
import numpy as np
from jax import lax
from jax.experimental import pallas as pl
from jax.experimental.pallas import tpu as pltpu

F32 = jnp.float32
BF16 = jnp.bfloat16

D_MODEL = 1024
CHUNK = 64
LEFT_CHUNKS = 8
BAND = LEFT_CHUNKS * CHUNK
N_HEADS = 8
HEAD_DIM = 64
WIDTH = N_HEADS * HEAD_DIM
REL_CLIP = 128
EPS = 1e-6
NEG_INF = -1e30
ATTN_SCALE = 0.125
IN_COLS = 4 * WIDTH + 4 * WIDTH + 2 * D_MODEL

LANES = 128
HEAD_PAIRS = N_HEADS // 2
EXP_ZERO_BELOW = -104.0
LOG2E = 1.4426950408889634
VMEM_LIMIT = 56 * 1024 * 1024


def _dot(a, b):
    return jnp.dot(a, b, preferred_element_type=F32)


def _dot_nt(a, b):
    return lax.dot_general(a, b, (((1,), (1,)), ((), ())), preferred_element_type=F32)


def _sigmoid(x):
    return 1.0 / (1.0 + jnp.exp(-x))


def _head_rms_norm(h, w128, lane_lo):
    sq = h * h
    s_lo = jnp.sum(jnp.where(lane_lo, sq, 0.0), axis=-1, keepdims=True)
    s_hi = jnp.sum(jnp.where(lane_lo, 0.0, sq), axis=-1, keepdims=True)
    r_lo = lax.rsqrt(s_lo * (1.0 / HEAD_DIM) + EPS)
    r_hi = lax.rsqrt(s_hi * (1.0 / HEAD_DIM) + EPS)
    return (h * jnp.where(lane_lo, r_lo, r_hi)) * w128


def _normed_bf16(x, nw):
    ms = jnp.mean(x * x, axis=-1, keepdims=True)
    return ((x * lax.rsqrt(ms + EPS)) * nw).astype(BF16)


def _proj_kernel(x_ref, nw_ref, w_ref, qnw_ref, knw_ref,
                 sbq_ref, sbk32_ref, sbk16_ref, sbv32_ref, sbv16_ref,
                 cbq_ref, cbk16_ref, cbv16_ref, cbk32_ref, cbv32_ref, *, pad_blocks):
    j = pl.program_id(1)

    if pad_blocks:
        @pl.when(j < pad_blocks)
        def _():
            for ref in (sbk16_ref, sbv16_ref, cbk16_ref, cbv16_ref):
                ref[...] = jnp.zeros(ref.shape, BF16)

    @pl.when(j >= pad_blocks)
    def _():
        xb = _normed_bf16(x_ref[0], nw_ref[...])
        tm = xb.shape[0]
        lane_lo = lax.broadcasted_iota(jnp.int32, (tm, LANES), 1) < HEAD_DIM

        def col(c0, width):
            return _dot(xb, w_ref[:, c0:c0 + width])

        sbq_ref[0] = (col(0, WIDTH) * ATTN_SCALE).astype(BF16)
        h = col(WIDTH, WIDTH)
        sbk32_ref[0] = h
        sbk16_ref[0] = h.astype(BF16)
        h = col(2 * WIDTH, WIDTH)
        sbv32_ref[0] = h
        sbv16_ref[0] = h.astype(BF16)

        hq_all = col(4 * WIDTH, WIDTH)
        hk_all = col(5 * WIDTH, WIDTH)
        for s in range(HEAD_PAIRS):
            sl = slice(s * LANES, (s + 1) * LANES)
            hq = _head_rms_norm(hq_all[:, sl], qnw_ref[...], lane_lo)
            cbq_ref[0, :, sl] = (hq * (ATTN_SCALE * LOG2E)).astype(BF16)
            hk = _head_rms_norm(hk_all[:, sl], knw_ref[...], lane_lo)
            cbk16_ref[0, :, sl] = hk.astype(BF16)
            cbk32_ref[0, :, sl] = hk

        h = col(6 * WIDTH, WIDTH)
        cbv16_ref[0] = h.astype(BF16)
        cbv32_ref[0] = h


def _project(x, nw, w_bf, qnw128, knw128, *, tm, pad_rows, tail):
    b, t, _ = x.shape
    nt = t // tm
    pad_blocks = pad_rows // tm
    tail_blocks = tail // tm
    first_tail_step = pad_blocks + nt - tail_blocks

    def row(bi, j):
        return (bi, jnp.maximum(j - pad_blocks, 0), 0)

    def padded(bi, j):
        return (bi, j, 0)

    def tail_map(bi, j):
        return (bi, jnp.maximum(j - first_tail_step, 0), 0)

    def const(bi, j):
        return (0, 0)

    def out(width, dtype, rows=t):
        return jax.ShapeDtypeStruct((b, rows, width), dtype)

    blk = lambda width, imap: pl.BlockSpec((1, tm, width), imap)
    out_shape = [
        out(WIDTH, BF16), out(WIDTH, F32), out(WIDTH, BF16, t + pad_rows),
        out(WIDTH, F32), out(WIDTH, BF16, t + pad_rows),
        out(WIDTH, BF16), out(WIDTH, BF16, t + pad_rows), out(WIDTH, BF16, t + pad_rows),
        out(WIDTH, F32, tail), out(WIDTH, F32, tail),
    ]
    out_specs = [
        blk(WIDTH, row), blk(WIDTH, row), blk(WIDTH, padded),
        blk(WIDTH, row), blk(WIDTH, padded),
        blk(WIDTH, row), blk(WIDTH, padded), blk(WIDTH, padded),
        blk(WIDTH, tail_map), blk(WIDTH, tail_map),
    ]
    in_specs = [
        blk(D_MODEL, row),
        pl.BlockSpec((1, D_MODEL), const),
        pl.BlockSpec((D_MODEL, IN_COLS), const, pipeline_mode=pl.Buffered(1)),
        pl.BlockSpec((1, LANES), const),
        pl.BlockSpec((1, LANES), const),
    ]
    return pl.pallas_call(
        functools.partial(_proj_kernel, pad_blocks=pad_blocks),
        grid=(b, nt + pad_blocks),
        in_specs=in_specs, out_specs=out_specs, out_shape=out_shape,
        compiler_params=pltpu.CompilerParams(
            dimension_semantics=("arbitrary", "arbitrary"), vmem_limit_bytes=VMEM_LIMIT),
        name="project",
    )(x, nw, w_bf, qnw128, knw128)


def _sb_kernel(*refs, blk, qtiles, new_pad, n_cache):
    if n_cache:
        (q_ref, kn_ref, vn_ref, kc_ref, vc_ref, uo_ref, o_ref,
         qm_all, acc_all, carry_all, z_all, sc_all) = refs
    else:
        (q_ref, kn_ref, vn_ref, uo_ref, o_ref,
         qm_all, acc_all, carry_all, z_all, sc_all) = refs
    lane_lo = lax.broadcasted_iota(jnp.int32, (blk, LANES), 1) < HEAD_DIM
    uo = uo_ref[...]
    row = lax.broadcasted_iota(jnp.int32, (blk, blk), 0)
    colm = lax.broadcasted_iota(jnp.int32, (blk, blk), 1)
    causal = colm < row
    head = lambda h: slice(h * HEAD_DIM, (h + 1) * HEAD_DIM)
    pair = lambda h: slice((h // 2) * LANES, (h // 2 + 1) * LANES)

    def visit(jobs, n, causal):
        sub = lambda u: slice(u * blk, (u + 1) * blk)
        masked = lambda u: causal is not None and u == n - 1
        work = [(t, scores, weighted, h) for t, scores, weighted in jobs for h in range(N_HEADS)]
        for t, scores, _, h in work:
            z_all[t, h, :, :n * blk] = scores(h, qm_all[t, h])
        for t, _, _, h in work:
            zh = z_all[t, h, :, :n * blk]
            sp = jnp.maximum(zh, 0.0) + jnp.log(1.0 + jnp.exp2(jnp.abs(zh) * (-LOG2E)))
            for u in range(n):
                spu = jnp.where(causal, sp[:, sub(u)], 0.0) if masked(u) else sp[:, sub(u)]
                hi = spu.astype(BF16)
                lo = (spu - hi.astype(F32)).astype(BF16)
                sc_all[t, h, u] = _dot(jnp.concatenate([hi, lo], axis=1), uo)
        for t, _, weighted, h in work:
            carry = carry_all[t, h]
            ws = [None] * n
            for u in reversed(range(n)):
                w = jnp.exp(z_all[t, h, :, sub(u)] - sc_all[t, h, u, :, :blk] - carry)
                if masked(u):
                    w = jnp.where(causal, w, 0.0)
                ws[u] = w.astype(BF16)
                carry = carry + sc_all[t, h, u, :, blk:]
            carry_all[t, h] = carry
            acc_all[t, h] += weighted(h, jnp.concatenate(ws, axis=1))
        return [jnp.min(carry_all[t]) < -EXP_ZERO_BELOW for t, _, _ in jobs]

    acc_all[...] = jnp.zeros(acc_all.shape, F32)
    carry_all[...] = jnp.zeros(carry_all.shape, F32)

    if new_pad is not None:
        blocks = [pl.program_id(1) * qtiles + t for t in range(qtiles)]
        diags = [pl.multiple_of(new_pad + ib * blk, blk) for ib in blocks]
        rows_of = lambda t: slice(t * blk, (t + 1) * blk)
        for t in range(qtiles):
            for hp in range(HEAD_PAIRS):
                q2 = q_ref[0, rows_of(t), pair(2 * hp)]
                zero = jnp.zeros_like(q2)
                qm_all[t, 2 * hp] = jnp.where(lane_lo, q2, zero)
                qm_all[t, 2 * hp + 1] = jnp.where(lane_lo, zero, q2)

        def new_rows(t, s, n):
            k_tile = kn_ref[0, pl.ds(s, n * blk), :]
            v_tile = vn_ref[0, pl.ds(s, n * blk), :]
            return (t, lambda h, q: _dot_nt(q, k_tile[:, pair(h)]),
                    lambda h, w: _dot(w, v_tile[:, pair(h)]))

        lives = [visit([new_rows(t, pl.multiple_of(diags[t] - 2 * blk, blk), 3)], 3, causal)[0]
                 for t in range(qtiles)]

        for t in range(qtiles):
            def older_new(c, t=t):
                j, _ = c
                s = pl.multiple_of(diags[t] - (j + 1) * 2 * blk, blk)
                return j + 1, visit([new_rows(t, s, 2)], 2, None)[0]

            def cond_new(c, t=t):
                j, live = c
                return jnp.logical_and(j < (blocks[t] + 1) // 2, live)
            lax.while_loop(cond_new, older_new, (jnp.int32(1), lives[t]))

        for t in range(qtiles):
            for hp in range(HEAD_PAIRS):
                o_ref[0, rows_of(t), pair(2 * hp)] = jnp.where(
                    lane_lo, acc_all[t, 2 * hp], acc_all[t, 2 * hp + 1]).astype(o_ref.dtype)
    else:
        for h in range(N_HEADS):
            qm_all[0, h] = q_ref[0, :, head(h)]

        def cache_cols(ref, h, s):
            return ref[0, h, :, pl.ds(s, 2 * blk)].astype(BF16)

        def cache_rows(s):
            return (lambda h, q: _dot(q, cache_cols(kc_ref, h, s)),
                    lambda h, w: _dot_nt(w, cache_cols(vc_ref, h, s)))

        top = (n_cache - 1) * 2 * blk
        old_scores, old_weighted = cache_rows(top)
        live, = visit(
            [(0,
              lambda h, q: jnp.concatenate(
                  [old_scores(h, q), _dot_nt(q, kn_ref[0, :, head(h)])], axis=1),
              lambda h, w: (old_weighted(h, w[:, :2 * blk])
                            + _dot(w[:, 2 * blk:], vn_ref[0, :, head(h)])))],
            3, causal)

        def older_cache(c):
            j, _ = c
            s = pl.multiple_of(j * 2 * blk, 2 * blk)
            return j - 1, visit([(0,) + cache_rows(s)], 2, None)[0]

        def cond_cache(c):
            j, live = c
            return jnp.logical_and(j >= 0, live)
        lax.while_loop(cond_cache, older_cache, (jnp.int32(n_cache - 2), live))

        for h in range(N_HEADS):
            o_ref[0, :, head(h)] = acc_all[0, h].astype(o_ref.dtype)


def _cumsum_matrix(blk):
    j = np.arange(blk)[:, None]
    s = np.arange(blk)[None, :]
    u = (j >= s).astype(np.float32)
    half = np.concatenate([u, np.ones((blk, blk), np.float32)], axis=1)
    return jnp.asarray(np.concatenate([half, half], axis=0), dtype=BF16)


def _stick_breaking(q, k_new, v_new, k_cache=None, v_cache=None, *, blk):
    b, tq, _ = q.shape
    nq = tq // blk
    new_pad = k_new.shape[1] - tq if nq > 1 else None
    assert (new_pad is None) == (k_cache is not None)
    assert new_pad is None or new_pad >= 2 * blk
    n_cache = 0 if k_cache is None else k_cache.shape[-1] // (2 * blk)
    d = HEAD_DIM if n_cache else LANES
    full = lambda arr: pl.BlockSpec((1,) + arr.shape[1:],
                                    lambda bi, i: (bi,) + (0,) * (arr.ndim - 1))
    qtiles = 2 if nq % 2 == 0 else 1
    qblk = pl.BlockSpec((1, qtiles * blk, WIDTH), lambda bi, i: (bi, i, 0))
    uo = _cumsum_matrix(blk)
    operands = [q, k_new, v_new]
    in_specs = [qblk, full(k_new), full(v_new)]
    if n_cache:
        operands += [k_cache, v_cache]
        in_specs += [full(k_cache), full(v_cache)]
    operands.append(uo)
    in_specs.append(pl.BlockSpec(uo.shape, lambda bi, i: (0, 0)))
    return pl.pallas_call(
        functools.partial(_sb_kernel, blk=blk, qtiles=qtiles, new_pad=new_pad, n_cache=n_cache),
        grid=(b, nq // qtiles),
        in_specs=in_specs, out_specs=qblk,
        out_shape=jax.ShapeDtypeStruct((b, tq, WIDTH), BF16),
        scratch_shapes=[pltpu.VMEM((qtiles, N_HEADS, blk, d), BF16),
                        pltpu.VMEM((qtiles, N_HEADS, blk, d), F32),
                        pltpu.VMEM((qtiles, N_HEADS, blk, blk), F32),
                        pltpu.VMEM((qtiles, N_HEADS, blk, 3 * blk), F32),
                        pltpu.VMEM((qtiles, N_HEADS, 3, blk, 2 * blk), F32)],
        compiler_params=pltpu.CompilerParams(
            dimension_semantics=("arbitrary", "arbitrary"), vmem_limit_bytes=VMEM_LIMIT),
        name="stick_breaking",
    )(*operands)


def _bias_kernel(rb_ref, o_ref, *, g):
    rows, win = g * CHUNK, (LEFT_CHUNKS + g) * CHUNK
    n = 768
    npad = rb_ref.shape[1]
    jj = lax.broadcasted_iota(jnp.int32, (npad, n), 1)
    rr = lax.broadcasted_iota(jnp.int32, (npad, n), 0)
    idx = jnp.where(jj > win, 2 * REL_CLIP,
                    jnp.clip(BAND - jj, -REL_CLIP, REL_CLIP) + REL_CLIP)
    sel = (rr == idx).astype(BF16)
    rb = rb_ref[...]
    hi = rb.astype(BF16)
    r1 = rb - hi.astype(F32)
    mid = r1.astype(BF16)
    lo = (r1 - mid.astype(F32)).astype(BF16)
    line = _dot(hi, sel) + _dot(mid, sel) + _dot(lo, sel)
    line = line * LOG2E
    tq = lax.broadcasted_iota(jnp.int32, (rows, win), 0)
    tk = lax.broadcasted_iota(jnp.int32, (rows, win), 1)
    off = tk - (tq // CHUNK) * CHUNK
    valid = jnp.logical_and(off >= 0, off < (LEFT_CHUNKS + 1) * CHUNK)
    for h in range(N_HEADS):
        tiled = jnp.broadcast_to(line[h:h + 1, :], (rows, n))
        rolled = pltpu.roll(tiled, 0, 1, stride=1, stride_axis=0)
        o_ref[h] = jnp.where(valid, rolled[:, :win], NEG_INF)


def _bias_table(rb_pad, g):
    rows, win = g * CHUNK, (LEFT_CHUNKS + g) * CHUNK
    return pl.pallas_call(
        functools.partial(_bias_kernel, g=g),
        out_shape=jax.ShapeDtypeStruct((N_HEADS, rows, win), F32),
        name="bias_table",
    )(rb_pad)


def _cb_kernel(q_ref, k_ref, v_ref, bias_ref, o_ref, *scratch, rows, win, tiles, pad_history):
    n_slots = tiles * HEAD_PAIRS
    s_refs, m_refs = scratch[:n_slots], scratch[n_slots:]
    p = pl.program_id(1)
    lane_lo = lax.broadcasted_iota(jnp.int32, (rows, LANES), 1) < HEAD_DIM
    zero_idx = jnp.minimum(p, 0)
    work = [(t, hp) for t in range(tiles) for hp in range(HEAD_PAIRS)]

    def body(mask_pad):
        starts = [pl.multiple_of((p * tiles + t) * rows, rows) for t in range(tiles)]
        for t, hp in work:
            sl = slice(hp * LANES, (hp + 1) * LANES)
            q2 = q_ref[0, t * rows:(t + 1) * rows, sl]
            zero = jnp.zeros_like(q2)
            qm = jnp.concatenate([jnp.where(lane_lo, q2, zero), jnp.where(lane_lo, zero, q2)],
                                 axis=0)
            s = _dot_nt(qm, k_ref[0, pl.ds(starts[t], win), sl]) + bias_ref[hp]
            if mask_pad:
                key_ok = (lax.broadcasted_iota(jnp.int32, (2 * rows, win), 1)
                          >= BAND - starts[t])
                s = jnp.where(key_ok, s, NEG_INF)
            s_refs[t * HEAD_PAIRS + hp][0] = s
            m_refs[t * HEAD_PAIRS + hp][0] = jnp.broadcast_to(
                jnp.max(s, axis=-1, keepdims=True), (2 * rows, LANES))
        for t, hp in work:
            sl = slice(hp * LANES, (hp + 1) * LANES)
            s = s_refs[t * HEAD_PAIRS + hp][zero_idx]
            m = m_refs[t * HEAD_PAIRS + hp][zero_idx]
            m = jnp.concatenate([m] * (win // LANES), axis=1) if win % LANES == 0 else m[:, :1]
            e = jnp.exp2(s - m)
            l = jnp.sum(e, axis=-1, keepdims=True)
            o = _dot(e.astype(BF16), v_ref[0, pl.ds(starts[t], win), sl]) / l
            o_ref[0, t * rows:(t + 1) * rows, sl] = jnp.where(
                lane_lo, o[:rows], o[rows:]).astype(o_ref.dtype)

    if pad_history:
        pad_steps = BAND // (rows * tiles)
        pl.when(p < pad_steps)(functools.partial(body, True))
        pl.when(p >= pad_steps)(functools.partial(body, False))
    else:
        body(False)


def _chunk_band(q, k_hist, v_hist, bias, *, g, tiles, pad_history):
    b, t, _ = q.shape
    rows, win = g * CHUNK, (LEFT_CHUNKS + g) * CHUNK
    assert BAND % (rows * tiles) == 0 and t % (rows * tiles) == 0
    full = lambda arr: pl.BlockSpec((1,) + arr.shape[1:], lambda bi, i: (bi, 0, 0))
    qblk = pl.BlockSpec((1, tiles * rows, WIDTH), lambda bi, i: (bi, i, 0))
    bias = bias.reshape(HEAD_PAIRS, 2 * rows, win)
    return pl.pallas_call(
        functools.partial(_cb_kernel, rows=rows, win=win, tiles=tiles, pad_history=pad_history),
        grid=(b, t // (rows * tiles)),
        in_specs=[qblk, full(k_hist), full(v_hist),
                  pl.BlockSpec(bias.shape, lambda bi, i: (0, 0, 0))],
        out_specs=qblk,
        out_shape=jax.ShapeDtypeStruct((b, t, WIDTH), BF16),
        scratch_shapes=([pltpu.VMEM((1, 2 * rows, win), F32)] * (tiles * HEAD_PAIRS)
                        + [pltpu.VMEM((1, 2 * rows, LANES), F32)] * (tiles * HEAD_PAIRS)),
        compiler_params=pltpu.CompilerParams(
            dimension_semantics=("arbitrary", "arbitrary"), vmem_limit_bytes=VMEM_LIMIT),
        name="chunk_band",
    )(q, k_hist, v_hist, bias)


def _merge_kernel(x_ref, osb_ref, ocb_ref, nw_ref, w_ref, wsb_ref, wcb_ref, wout_ref, y_ref):
    x = x_ref[0]
    xb = _normed_bf16(x, nw_ref[...])

    def col(c0, width):
        return _dot(xb, w_ref[:, c0:c0 + width])

    zsb = col(3 * WIDTH, WIDTH)
    zcb = col(7 * WIDTH, WIDTH)
    a_sb = (osb_ref[0].astype(F32) * (zsb * _sigmoid(zsb))).astype(BF16)
    a_cb = (ocb_ref[0].astype(F32) * (zcb * _sigmoid(zcb))).astype(BF16)
    b_sb = _dot(a_sb, wsb_ref[...])
    b_cb = _dot(a_cb, wcb_ref[...])
    h = (_sigmoid(col(8 * WIDTH, D_MODEL)) * b_sb
         + _sigmoid(col(8 * WIDTH + D_MODEL, D_MODEL)) * b_cb)
    y_ref[0] = x + _dot(h.astype(BF16), wout_ref[...])


def _merge(x, osb, ocb, nw, w_bf, wsb, wcb, wout, *, tm):
    b, t, _ = x.shape
    wide = pl.BlockSpec((1, tm, D_MODEL), lambda bi, i: (bi, i, 0))
    narrow = pl.BlockSpec((1, tm, WIDTH), lambda bi, i: (bi, i, 0))
    const = lambda arr: pl.BlockSpec(arr.shape, lambda bi, i: (0, 0),
                                     pipeline_mode=pl.Buffered(1))
    return pl.pallas_call(
        _merge_kernel,
        grid=(b, t // tm),
        in_specs=[wide, narrow, narrow, const(nw), const(w_bf),
                  const(wsb), const(wcb), const(wout)],
        out_specs=wide,
        out_shape=jax.ShapeDtypeStruct((b, t, D_MODEL), F32),
        compiler_params=pltpu.CompilerParams(
            dimension_semantics=("arbitrary", "arbitrary"), vmem_limit_bytes=VMEM_LIMIT),
        name="merge",
    )(x, osb, ocb, nw, w_bf, wsb, wcb, wout)


def _heads(a):
    b, t, _ = a.shape
    return a.reshape(b, t, N_HEADS, HEAD_DIM)


def kernel(x_prompt, x_sample, cache_sb_k, cache_sb_v, cache_cb_k, cache_cb_v, norm_w, w_in,
           q_norm_w, k_norm_w, rel_bias, w_proj_sb, w_proj_cb, w_out):
    depth = w_in.shape[0]
    b, t, _ = x_prompt.shape
    bs, tn, _ = x_sample.shape
    p = cache_sb_k.shape[2]
    r = cache_cb_k.shape[2]
    assert t % (2 * CHUNK) == 0 and tn == CHUNK and r == BAND and p % (2 * tn) == 0

    y_p, y_s = x_prompt, x_sample
    outs = [[] for _ in range(8)]
    for l in range(depth):
        nw = norm_w[l].reshape(1, D_MODEL)
        w_bf = w_in[l].astype(BF16)
        qnw = jnp.tile(q_norm_w[l], 2).reshape(1, LANES)
        knw = jnp.tile(k_norm_w[l], 2).reshape(1, LANES)
        rb_pad = jnp.pad(rel_bias[l], ((0, 0), (0, 384 - (2 * REL_CLIP + 1))))
        bias2 = _bias_table(rb_pad, 2)
        bias1 = bias2[:, :CHUNK, :(LEFT_CHUNKS + 1) * CHUNK]
        wsb = w_proj_sb[l].astype(BF16)
        wcb = w_proj_cb[l].astype(BF16)
        wout = w_out[l].astype(BF16)

        (sbq, sbk32, sbk16, sbv32, sbv16, cbq, cbk16, cbv16, cbk32, cbv32) = _project(
            y_p, nw, w_bf, qnw, knw, tm=512, pad_rows=BAND, tail=min(BAND, t))
        o_sb = _stick_breaking(sbq, sbk16, sbv16, blk=128)
        o_cb = _chunk_band(cbq, cbk16, cbv16, bias2, g=2, tiles=2, pad_history=True)
        y_p_next = _merge(y_p, o_sb, o_cb, nw, w_bf, wsb, wcb, wout, tm=512)
        for lst, a in zip(outs[:4], (sbk32, sbv32, cbk32, cbv32)):
            lst.append(_heads(a))

        proj = _project(y_s.reshape(1, bs * tn, D_MODEL), nw, w_bf, qnw, knw,
                        tm=bs * tn, pad_rows=0, tail=bs * tn)
        (sbq, sbk32, sbk16, sbv32, sbv16, cbq, cbk16, cbv16, cbk32, cbv32) = [
            a.reshape(bs, tn, a.shape[-1]) for a in proj]
        o_sb = _stick_breaking(sbq, sbk16, sbv16,
                               jnp.transpose(cache_sb_k[l], (0, 2, 3, 1)),
                               jnp.transpose(cache_sb_v[l], (0, 2, 3, 1)), blk=tn)
        k_hist = jnp.concatenate([cache_cb_k[l].reshape(bs, r, WIDTH).astype(BF16), cbk16], axis=1)
        v_hist = jnp.concatenate([cache_cb_v[l].reshape(bs, r, WIDTH).astype(BF16), cbv16], axis=1)
        o_cb = _chunk_band(cbq, k_hist, v_hist, bias1, g=1, tiles=1, pad_history=False)
        flat = lambda a: a.reshape(1, bs * tn, a.shape[-1])
        y_s_next = _merge(flat(y_s), flat(o_sb), flat(o_cb), nw, w_bf, wsb, wcb, wout,
                          tm=bs * tn).reshape(bs, tn, D_MODEL)
        for lst, a in zip(outs[4:], (sbk32, sbv32, cbk32, cbv32)):
            lst.append(_heads(a))

        y_p, y_s = y_p_next, y_s_next

    return (y_p, y_s) + tuple(jnp.stack(o) for o in outs)
```

```python
import functools

import jax
import jax.numpy as jnp
import numpy as np
from jax import lax
from jax.experimental import pallas as pl
from jax.experimental.pallas import tpu as pltpu

F32 = jnp.float32
BF16 = jnp.bfloat16

D_MODEL = 1024
CHUNK = 64
LEFT_CHUNKS = 8
BAND = LEFT_CHUNKS * CHUNK
N_HEADS = 8
HEAD_DIM = 64
WIDTH = N_HEADS * HEAD_DIM
REL_CLIP = 128
EPS = 1e-6
NEG_INF = -1e30
ATTN_SCALE = 0.125
IN_COLS = 4 * WIDTH + 4 * WIDTH + 2 * D_MODEL

LANES = 128
HEAD_PAIRS = N_HEADS // 2
EXP_ZERO_BELOW = -104.0
LOG2E = 1.4426950408889634
VMEM_LIMIT = 56 * 1024 * 1024


def _dot(a, b):
    return jnp.dot(a, b, preferred_element_type=F32)


def _dot_nt(a, b):
    return lax.dot_general(a, b, (((1,), (1,)), ((), ())), preferred_element_type=F32)


def _sigmoid(x):
    return 1.0 / (1.0 + jnp.exp(-x))


def _head_rms_norm(h, w128, lane_lo):
    sq = h * h
    s_lo = jnp.sum(jnp.where(lane_lo, sq, 0.0), axis=-1, keepdims=True)
    s_hi = jnp.sum(jnp.where(lane_lo, 0.0, sq), axis=-1, keepdims=True)
    r_lo = lax.rsqrt(s_lo * (1.0 / HEAD_DIM) + EPS)
    r_hi = lax.rsqrt(s_hi * (1.0 / HEAD_DIM) + EPS)
    return (h * jnp.where(lane_lo, r_lo, r_hi)) * w128


def _normed_bf16(x, nw):
    ms = jnp.mean(x * x, axis=-1, keepdims=True)
    return ((x * lax.rsqrt(ms + EPS)) * nw).astype(BF16)


def _proj_kernel(x_ref, nw_ref, w_ref, qnw_ref, knw_ref,
                 sbq_ref, sbk32_ref, sbk16_ref, sbv32_ref, sbv16_ref,
                 cbq_ref, cbk16_ref, cbv16_ref, cbk32_ref, cbv32_ref, *, pad_blocks):
    j = pl.program_id(1)

    if pad_blocks:
        @pl.when(j < pad_blocks)
        def _():
            for ref in (sbk16_ref, sbv16_ref, cbk16_ref, cbv16_ref):
                ref[...] = jnp.zeros(ref.shape, BF16)

    @pl.when(j >= pad_blocks)
    def _():
        xb = _normed_bf16(x_ref[0], nw_ref[...])
        tm = xb.shape[0]
        lane_lo = lax.broadcasted_iota(jnp.int32, (tm, LANES), 1) < HEAD_DIM

        def col(c0, width):
            return _dot(xb, w_ref[:, c0:c0 + width])

        sbq_ref[0] = (col(0, WIDTH) * ATTN_SCALE).astype(BF16)
        h = col(WIDTH, WIDTH)
        sbk32_ref[0] = h
        sbk16_ref[0] = h.astype(BF16)
        h = col(2 * WIDTH, WIDTH)
        sbv32_ref[0] = h
        sbv16_ref[0] = h.astype(BF16)

        hq_all = col(4 * WIDTH, WIDTH)
        hk_all = col(5 * WIDTH, WIDTH)
        for s in range(HEAD_PAIRS):
            sl = slice(s * LANES, (s + 1) * LANES)
            hq = _head_rms_norm(hq_all[:, sl], qnw_ref[...], lane_lo)
            cbq_ref[0, :, sl] = (hq * (ATTN_SCALE * LOG2E)).astype(BF16)
            hk = _head_rms_norm(hk_all[:, sl], knw_ref[...], lane_lo)
            cbk16_ref[0, :, sl] = hk.astype(BF16)
            cbk32_ref[0, :, sl] = hk

        h = col(6 * WIDTH, WIDTH)
        cbv16_ref[0] = h.astype(BF16)
        cbv32_ref[0] = h


def _project(x, nw, w_bf, qnw128, knw128, *, tm, pad_rows, tail):
    b, t, _ = x.shape
    nt = t // tm
    pad_blocks = pad_rows // tm
    tail_blocks = tail // tm
    first_tail_step = pad_blocks + nt - tail_blocks

    def row(bi, j):
        return (bi, jnp.maximum(j - pad_blocks, 0), 0)

    def padded(bi, j):
        return (bi, j, 0)

    def tail_map(bi, j):
        return (bi, jnp.maximum(j - first_tail_step, 0), 0)

    def const(bi, j):
        return (0, 0)

    def out(width, dtype, rows=t):
        return jax.ShapeDtypeStruct((b, rows, width), dtype)

    blk = lambda width, imap: pl.BlockSpec((1, tm, width), imap)
    out_shape = [
        out(WIDTH, BF16), out(WIDTH, F32), out(WIDTH, BF16, t + pad_rows),
        out(WIDTH, F32), out(WIDTH, BF16, t + pad_rows),
        out(WIDTH, BF16), out(WIDTH, BF16, t + pad_rows), out(WIDTH, BF16, t + pad_rows),
        out(WIDTH, F32, tail), out(WIDTH, F32, tail),
    ]
    out_specs = [
        blk(WIDTH, row), blk(WIDTH, row), blk(WIDTH, padded),
        blk(WIDTH, row), blk(WIDTH, padded),
        blk(WIDTH, row), blk(WIDTH, padded), blk(WIDTH, padded),
        blk(WIDTH, tail_map), blk(WIDTH, tail_map),
    ]
    in_specs = [
        blk(D_MODEL, row),
        pl.BlockSpec((1, D_MODEL), const),
        pl.BlockSpec((D_MODEL, IN_COLS), const, pipeline_mode=pl.Buffered(1)),
        pl.BlockSpec((1, LANES), const),
        pl.BlockSpec((1, LANES), const),
    ]
    return pl.pallas_call(
        functools.partial(_proj_kernel, pad_blocks=pad_blocks),
        grid=(b, nt + pad_blocks),
        in_specs=in_specs, out_specs=out_specs, out_shape=out_shape,
        compiler_params=pltpu.CompilerParams(
            dimension_semantics=("arbitrary", "arbitrary"), vmem_limit_bytes=VMEM_LIMIT),
        name="project",
    )(x, nw, w_bf, qnw128, knw128)


def _sb_kernel(*refs, blk, qtiles, new_pad, n_cache):
    if n_cache:
        (q_ref, kn_ref, vn_ref, kc_ref, vc_ref, uo_ref, o_ref,
         qm_all, acc_all, carry_all, z_all, sc_all) = refs
    else:
        (q_ref, kn_ref, vn_ref, uo_ref, o_ref,
         qm_all, acc_all, carry_all, z_all, sc_all) = refs
    lane_lo = lax.broadcasted_iota(jnp.int32, (blk, LANES), 1) < HEAD_DIM
    uo = uo_ref[...]
    row = lax.broadcasted_iota(jnp.int32, (blk, blk), 0)
    colm = lax.broadcasted_iota(jnp.int32, (blk, blk), 1)
    causal = colm < row
    head = lambda h: slice(h * HEAD_DIM, (h + 1) * HEAD_DIM)
    pair = lambda h: slice((h // 2) * LANES, (h // 2 + 1) * LANES)

    def visit(jobs, n, causal):
        sub = lambda u: slice(u * blk, (u + 1) * blk)
        masked = lambda u: causal is not None and u == n - 1
        work = [(t, scores, weighted, h) for t, scores, weighted in jobs for h in range(N_HEADS)]
        for t, scores, _, h in work:
            z_all[t, h, :, :n * blk] = scores(h, qm_all[t, h])
        for t, _, _, h in work:
            zh = z_all[t, h, :, :n * blk]
            sp = jnp.maximum(zh, 0.0) + jnp.log(1.0 + jnp.exp2(jnp.abs(zh) * (-LOG2E)))
            for u in range(n):
                spu = jnp.where(causal, sp[:, sub(u)], 0.0) if masked(u) else sp[:, sub(u)]
                hi = spu.astype(BF16)
                lo = (spu - hi.astype(F32)).astype(BF16)
                sc_all[t, h, u] = _dot(jnp.concatenate([hi, lo], axis=1), uo)
        for t, _, weighted, h in work:
            carry = carry_all[t, h]
            ws = [None] * n
            for u in reversed(range(n)):
                w = jnp.exp(z_all[t, h, :, sub(u)] - sc_all[t, h, u, :, :blk] - carry)
                if masked(u):
                    w = jnp.where(causal, w, 0.0)
                ws[u] = w.astype(BF16)
                carry = carry + sc_all[t, h, u, :, blk:]
            carry_all[t, h] = carry
            acc_all[t, h] += weighted(h, jnp.concatenate(ws, axis=1))
        return [jnp.min(carry_all[t]) < -EXP_ZERO_BELOW for t, _, _ in jobs]

    acc_all[...] = jnp.zeros(acc_all.shape, F32)
    carry_all[...] = jnp.zeros(carry_all.shape, F32)

    if new_pad is not None:
        blocks = [pl.program_id(1) * qtiles + t for t in range(qtiles)]
        diags = [pl.multiple_of(new_pad + ib * blk, blk) for ib in blocks]
        rows_of = lambda t: slice(t * blk, (t + 1) * blk)
        for t in range(qtiles):
            for hp in range(HEAD_PAIRS):
                q2 = q_ref[0, rows_of(t), pair(2 * hp)]
                zero = jnp.zeros_like(q2)
                qm_all[t, 2 * hp] = jnp.where(lane_lo, q2, zero)
                qm_all[t, 2 * hp + 1] = jnp.where(lane_lo, zero, q2)

        def new_rows(t, s, n):
            k_tile = kn_ref[0, pl.ds(s, n * blk), :]
            v_tile = vn_ref[0, pl.ds(s, n * blk), :]
            return (t, lambda h, q: _dot_nt(q, k_tile[:, pair(h)]),
                    lambda h, w: _dot(w, v_tile[:, pair(h)]))

        lives = [visit([new_rows(t, pl.multiple_of(diags[t] - 2 * blk, blk), 3)], 3, causal)[0]
                 for t in range(qtiles)]

        for t in range(qtiles):
            def older_new(c, t=t):
                j, _ = c
                s = pl.multiple_of(diags[t] - (j + 1) * 2 * blk, blk)
                return j + 1, visit([new_rows(t, s, 2)], 2, None)[0]

            def cond_new(c, t=t):
                j, live = c
                return jnp.logical_and(j < (blocks[t] + 1) // 2, live)
            lax.while_loop(cond_new, older_new, (jnp.int32(1), lives[t]))

        for t in range(qtiles):
            for hp in range(HEAD_PAIRS):
                o_ref[0, rows_of(t), pair(2 * hp)] = jnp.where(
                    lane_lo, acc_all[t, 2 * hp], acc_all[t, 2 * hp + 1]).astype(o_ref.dtype)
    else:
        for h in range(N_HEADS):
            qm_all[0, h] = q_ref[0, :, head(h)]

        def cache_cols(ref, h, s):
            return ref[0, h, :, pl.ds(s, 2 * blk)].astype(BF16)

        def cache_rows(s):
            return (lambda h, q: _dot(q, cache_cols(kc_ref, h, s)),
                    lambda h, w: _dot_nt(w, cache_cols(vc_ref, h, s)))

        top = (n_cache - 1) * 2 * blk
        old_scores, old_weighted = cache_rows(top)
        live, = visit(
            [(0,
              lambda h, q: jnp.concatenate(
                  [old_scores(h, q), _dot_nt(q, kn_ref[0, :, head(h)])], axis=1),
              lambda h, w: (old_weighted(h, w[:, :2 * blk])
                            + _dot(w[:, 2 * blk:], vn_ref[0, :, head(h)])))],
            3, causal)

        def older_cache(c):
            j, _ = c
            s = pl.multiple_of(j * 2 * blk, 2 * blk)
            return j - 1, visit([(0,) + cache_rows(s)], 2, None)[0]

        def cond_cache(c):
            j, live = c
            return jnp.logical_and(j >= 0, live)
        lax.while_loop(cond_cache, older_cache, (jnp.int32(n_cache - 2), live))

        for h in range(N_HEADS):
            o_ref[0, :, head(h)] = acc_all[0, h].astype(o_ref.dtype)


def _cumsum_matrix(blk):
    j = np.arange(blk)[:, None]
    s = np.arange(blk)[None, :]
    u = (j >= s).astype(np.float32)
    half = np.concatenate([u, np.ones((blk, blk), np.float32)], axis=1)
    return jnp.asarray(np.concatenate([half, half], axis=0), dtype=BF16)


def _stick_breaking(q, k_new, v_new, k_cache=None, v_cache=None, *, blk):
    b, tq, _ = q.shape
    nq = tq // blk
    new_pad = k_new.shape[1] - tq if nq > 1 else None
    assert (new_pad is None) == (k_cache is not None)
    assert new_pad is None or new_pad >= 2 * blk
    n_cache = 0 if k_cache is None else k_cache.shape[-1] // (2 * blk)
    d = HEAD_DIM if n_cache else LANES
    full = lambda arr: pl.BlockSpec((1,) + arr.shape[1:],
                                    lambda bi, i: (bi,) + (0,) * (arr.ndim - 1))
    qtiles = 2 if nq % 2 == 0 else 1
    qblk = pl.BlockSpec((1, qtiles * blk, WIDTH), lambda bi, i: (bi, i, 0))
    uo = _cumsum_matrix(blk)
    operands = [q, k_new, v_new]
    in_specs = [qblk, full(k_new), full(v_new)]
    if n_cache:
        operands += [k_cache, v_cache]
        in_specs += [full(k_cache), full(v_cache)]
    operands.append(uo)
    in_specs.append(pl.BlockSpec(uo.shape, lambda bi, i: (0, 0)))
    return pl.pallas_call(
        functools.partial(_sb_kernel, blk=blk, qtiles=qtiles, new_pad=new_pad, n_cache=n_cache),
        grid=(b, nq // qtiles),
        in_specs=in_specs, out_specs=qblk,
        out_shape=jax.ShapeDtypeStruct((b, tq, WIDTH), BF16),
        scratch_shapes=[pltpu.VMEM((qtiles, N_HEADS, blk, d), BF16),
                        pltpu.VMEM((qtiles, N_HEADS, blk, d), F32),
                        pltpu.VMEM((qtiles, N_HEADS, blk, blk), F32),
                        pltpu.VMEM((qtiles, N_HEADS, blk, 3 * blk), F32),
                        pltpu.VMEM((qtiles, N_HEADS, 3, blk, 2 * blk), F32)],
        compiler_params=pltpu.CompilerParams(
            dimension_semantics=("arbitrary", "arbitrary"), vmem_limit_bytes=VMEM_LIMIT),
        name="stick_breaking",
    )(*operands)


def _bias_kernel(rb_ref, o_ref, *, g):
    rows, win = g * CHUNK, (LEFT_CHUNKS + g) * CHUNK
    n = 768
    npad = rb_ref.shape[1]
    jj = lax.broadcasted_iota(jnp.int32, (npad, n), 1)
    rr = lax.broadcasted_iota(jnp.int32, (npad, n), 0)
    idx = jnp.where(jj > win, 2 * REL_CLIP,
                    jnp.clip(BAND - jj, -REL_CLIP, REL_CLIP) + REL_CLIP)
    sel = (rr == idx).astype(BF16)
    rb = rb_ref[...]
    hi = rb.astype(BF16)
    r1 = rb - hi.astype(F32)
    mid = r1.astype(BF16)
    lo = (r1 - mid.astype(F32)).astype(BF16)
    line = _dot(hi, sel) + _dot(mid, sel) + _dot(lo, sel)
    line = line * LOG2E
    tq = lax.broadcasted_iota(jnp.int32, (rows, win), 0)
    tk = lax.broadcasted_iota(jnp.int32, (rows, win), 1)
    off = tk - (tq // CHUNK) * CHUNK
    valid = jnp.logical_and(off >= 0, off < (LEFT_CHUNKS + 1) * CHUNK)
    for h in range(N_HEADS):
        tiled = jnp.broadcast_to(line[h:h + 1, :], (rows, n))
        rolled = pltpu.roll(tiled, 0, 1, stride=1, stride_axis=0)
        o_ref[h] = jnp.where(valid, rolled[:, :win], NEG_INF)


def _bias_table(rb_pad, g):
    rows, win = g * CHUNK, (LEFT_CHUNKS + g) * CHUNK
    return pl.pallas_call(
        functools.partial(_bias_kernel, g=g),
        out_shape=jax.ShapeDtypeStruct((N_HEADS, rows, win), F32),
        name="bias_table",
    )(rb_pad)


def _band_tiles(q_ref, k_ref, v_ref, bias_ref, store, s_refs, m_refs, zero_idx, *,
                rows, win, tile_rows, tile_starts, mask_pad):
    lane_lo = lax.broadcasted_iota(jnp.int32, (rows, LANES), 1) < HEAD_DIM
    work = [(t, hp) for t in range(len(tile_rows)) for hp in range(HEAD_PAIRS)]

    def first(t, hp):
        sl = slice(hp * LANES, (hp + 1) * LANES)
        q2 = q_ref[0, tile_rows[t], sl]
        zero = jnp.zeros_like(q2)
        qm = jnp.concatenate([jnp.where(lane_lo, q2, zero), jnp.where(lane_lo, zero, q2)],
                             axis=0)
        s = _dot_nt(qm, k_ref[0, pl.ds(tile_starts[t], win), sl]) + bias_ref[hp]
        if mask_pad:
            key_ok = (lax.broadcasted_iota(jnp.int32, (2 * rows, win), 1)
                      >= BAND - tile_starts[t])
            s = jnp.where(key_ok, s, NEG_INF)
        s_refs[t * HEAD_PAIRS + hp][0] = s
        m_refs[t * HEAD_PAIRS + hp][0] = jnp.broadcast_to(
            jnp.max(s, axis=-1, keepdims=True), (2 * rows, LANES))

    def second(t, hp):
        sl = slice(hp * LANES, (hp + 1) * LANES)
        s = s_refs[t * HEAD_PAIRS + hp][zero_idx]
        m = m_refs[t * HEAD_PAIRS + hp][zero_idx]
        m = jnp.concatenate([m] * (win // LANES), axis=1) if win % LANES == 0 else m[:, :1]
        e = jnp.exp2(s - m)
        l = jnp.sum(e, axis=-1, keepdims=True)
        o = _dot(e.astype(BF16), v_ref[0, pl.ds(tile_starts[t], win), sl]) / l
        store(t, hp, jnp.where(lane_lo, o[:rows], o[rows:]))

    return ([functools.partial(first, t, hp) for t, hp in work]
            + [functools.partial(second, t, hp) for t, hp in work])


def _cb_kernel(q_ref, k_ref, v_ref, bias_ref, o_ref, *scratch, rows, win, tiles, pad_history):
    n_slots = tiles * HEAD_PAIRS
    s_refs, m_refs = scratch[:n_slots], scratch[n_slots:]
    p = pl.program_id(1)
    tile_rows = [slice(t * rows, (t + 1) * rows) for t in range(tiles)]

    def store(t, hp, value):
        o_ref[0, tile_rows[t], hp * LANES:(hp + 1) * LANES] = value.astype(o_ref.dtype)

    def body(mask_pad):
        starts = [pl.multiple_of((p * tiles + t) * rows, rows) for t in range(tiles)]
        for item in _band_tiles(q_ref, k_ref, v_ref, bias_ref, store, s_refs, m_refs,
                                jnp.minimum(p, 0), rows=rows, win=win, tile_rows=tile_rows,
                                tile_starts=starts, mask_pad=mask_pad):
            item()

    if pad_history:
        pad_steps = BAND // (rows * tiles)
        pl.when(p < pad_steps)(functools.partial(body, True))
        pl.when(p >= pad_steps)(functools.partial(body, False))
    else:
        body(False)


def _chunk_band(q, k_hist, v_hist, bias, *, g, tiles, pad_history):
    b, t, _ = q.shape
    rows, win = g * CHUNK, (LEFT_CHUNKS + g) * CHUNK
    assert BAND % (rows * tiles) == 0 and t % (rows * tiles) == 0
    full = lambda arr: pl.BlockSpec((1,) + arr.shape[1:], lambda bi, i: (bi, 0, 0))
    qblk = pl.BlockSpec((1, tiles * rows, WIDTH), lambda bi, i: (bi, i, 0))
    bias = bias.reshape(HEAD_PAIRS, 2 * rows, win)
    return pl.pallas_call(
        functools.partial(_cb_kernel, rows=rows, win=win, tiles=tiles, pad_history=pad_history),
        grid=(b, t // (rows * tiles)),
        in_specs=[qblk, full(k_hist), full(v_hist),
                  pl.BlockSpec(bias.shape, lambda bi, i: (0, 0, 0))],
        out_specs=qblk,
        out_shape=jax.ShapeDtypeStruct((b, t, WIDTH), BF16),
        scratch_shapes=([pltpu.VMEM((1, 2 * rows, win), F32)] * (tiles * HEAD_PAIRS)
                        + [pltpu.VMEM((1, 2 * rows, LANES), F32)] * (tiles * HEAD_PAIRS)),
        compiler_params=pltpu.CompilerParams(
            dimension_semantics=("arbitrary", "arbitrary"), vmem_limit_bytes=VMEM_LIMIT),
        name="chunk_band",
    )(q, k_hist, v_hist, bias)


def _merge_rows(x, osb, ocb, nw, wzsb, wzcb, wg, wsb, wcb, wout):
    xb = _normed_bf16(x, nw)
    zsb = _dot(xb, wzsb)
    zcb = _dot(xb, wzcb)
    a_sb = (osb.astype(F32) * (zsb * _sigmoid(zsb))).astype(BF16)
    a_cb = (ocb.astype(F32) * (zcb * _sigmoid(zcb))).astype(BF16)
    b_sb = _dot(a_sb, wsb)
    b_cb = _dot(a_cb, wcb)
    g = _dot(xb, wg)
    h = _sigmoid(g[:, :D_MODEL]) * b_sb + _sigmoid(g[:, D_MODEL:]) * b_cb
    return x + _dot(h.astype(BF16), wout)


def _merge_kernel(x_ref, osb_ref, ocb_ref, nw_ref, wzsb_ref, wzcb_ref, wg_ref,
                  wsb_ref, wcb_ref, wout_ref, y_ref):
    y_ref[0] = _merge_rows(x_ref[0], osb_ref[0], ocb_ref[0], nw_ref[...], wzsb_ref[...],
                           wzcb_ref[...], wg_ref[...], wsb_ref[...], wcb_ref[...],
                           wout_ref[...])


def _resident(shape, index):
    return pl.BlockSpec(shape, lambda *_: index, pipeline_mode=pl.Buffered(1))


def _merge_weight_specs():
    return [_resident((1, D_MODEL), (0, 0)),
            _resident((D_MODEL, WIDTH), (0, 3)), _resident((D_MODEL, WIDTH), (0, 7)),
            _resident((D_MODEL, 2 * D_MODEL), (0, 2)),
            _resident((WIDTH, D_MODEL), (0, 0)), _resident((WIDTH, D_MODEL), (0, 0)),
            _resident((D_MODEL, D_MODEL), (0, 0))]


def _merge(x, osb, ocb, nw, w_bf, wsb, wcb, wout, *, tm):
    b, t, _ = x.shape
    wide = pl.BlockSpec((1, tm, D_MODEL), lambda bi, i: (bi, i, 0))
    narrow = pl.BlockSpec((1, tm, WIDTH), lambda bi, i: (bi, i, 0))
    return pl.pallas_call(
        _merge_kernel,
        grid=(b, t // tm),
        in_specs=[wide, narrow, narrow] + _merge_weight_specs(),
        out_specs=wide,
        out_shape=jax.ShapeDtypeStruct((b, t, D_MODEL), F32),
        compiler_params=pltpu.CompilerParams(
            dimension_semantics=("arbitrary", "arbitrary"), vmem_limit_bytes=VMEM_LIMIT),
        name="merge",
    )(x, osb, ocb, nw, w_bf, w_bf, w_bf, wsb, wcb, wout)


def _band_merge_kernel(q_ref, k_ref, v_ref, bias_ref, x_ref, osb_ref, nw_ref, wzsb_ref,
                       wzcb_ref, wg_ref, wsb_ref, wcb_ref, wout_ref, y_ref, ocb_ref, *scratch,
                       rows, win, group, tiles_per_row_tile, row_tiles):
    n_slots = group * HEAD_PAIRS
    xb_ref, zg_ref = scratch[:2]
    s_refs, m_refs = scratch[2:2 + n_slots], scratch[2 + n_slots:]
    s = pl.program_id(0)
    slot = s % 2
    i = jnp.minimum(s, pl.num_programs(0) - 2) % row_tiles

    @pl.when(s == 0)
    def _():
        ocb_ref[...] = jnp.zeros(ocb_ref.shape, ocb_ref.dtype)

    def normalise():
        xb_ref[...] = _normed_bf16(x_ref[0], nw_ref[...])

    def gate_cols(c0, width, w_ref, w0):
        def run():
            zg_ref[:, c0:c0 + width] = _dot(xb_ref[...], w_ref[:, w0:w0 + width])
        return run

    def finish():
        zsb = zg_ref[:, :WIDTH]
        zcb = zg_ref[:, WIDTH:2 * WIDTH]
        a_sb = (osb_ref[0].astype(F32) * (zsb * _sigmoid(zsb))).astype(BF16)
        a_cb = (ocb_ref[1 - slot].astype(F32) * (zcb * _sigmoid(zcb))).astype(BF16)
        h = (_sigmoid(zg_ref[:, 2 * WIDTH:2 * WIDTH + D_MODEL]) * _dot(a_sb, wsb_ref[...])
             + _sigmoid(zg_ref[:, 2 * WIDTH + D_MODEL:]) * _dot(a_cb, wcb_ref[...]))
        y_ref[0] = x_ref[0] + _dot(h.astype(BF16), wout_ref[...])

    merge_items = [normalise,
                   gate_cols(0, WIDTH, wzsb_ref, 0), gate_cols(WIDTH, WIDTH, wzcb_ref, 0),
                   gate_cols(2 * WIDTH, D_MODEL, wg_ref, 0),
                   gate_cols(2 * WIDTH + D_MODEL, D_MODEL, wg_ref, D_MODEL), finish]

    def body(mask_pad):
        band_items = []
        for g0 in range(0, tiles_per_row_tile, group):
            tile_rows = [slice((g0 + t) * rows, (g0 + t + 1) * rows) for t in range(group)]
            starts = [pl.multiple_of((i * tiles_per_row_tile + g0 + t) * rows, rows)
                      for t in range(group)]

            def store(t, hp, value, tile_rows=tile_rows):
                ocb_ref[slot, tile_rows[t], hp * LANES:(hp + 1) * LANES] = value.astype(
                    ocb_ref.dtype)

            band_items += _band_tiles(q_ref, k_ref, v_ref, bias_ref, store, s_refs, m_refs,
                                      jnp.minimum(s, 0), rows=rows, win=win,
                                      tile_rows=tile_rows, tile_starts=starts,
                                      mask_pad=mask_pad)
        every = len(band_items) // len(merge_items)
        for n, item in enumerate(band_items):
            if n % every == 0 and n // every < len(merge_items):
                merge_items[n // every]()
            item()

    pl.when(i == 0)(functools.partial(body, True))
    pl.when(i != 0)(functools.partial(body, False))


def _band_merge(q, k_hist, v_hist, bias, x, osb, nw, w_bf, wsb, wcb, wout, *, g, group, tm):
    b, t, _ = q.shape
    rows, win = g * CHUNK, (LEFT_CHUNKS + g) * CHUNK
    row_tiles = t // tm
    tiles_per_row_tile = tm // rows
    assert t % tm == 0 and tm % (rows * group) == 0 and BAND <= tm
    steps = b * row_tiles + 1
    last = b * row_tiles - 1

    def cur(s):
        c = jnp.minimum(s, last)
        return c // row_tiles, c % row_tiles

    def prev(s):
        c = jnp.maximum(s - 1, 0)
        return c // row_tiles, c % row_tiles

    q_spec = pl.BlockSpec((1, tm, WIDTH), lambda s: cur(s) + (0,))
    hist = pl.BlockSpec((1,) + k_hist.shape[1:], lambda s: (cur(s)[0], 0, 0),
                        pipeline_mode=pl.Buffered(1))
    wide = pl.BlockSpec((1, tm, D_MODEL), lambda s: prev(s) + (0,))
    narrow = pl.BlockSpec((1, tm, WIDTH), lambda s: prev(s) + (0,))
    bias = bias.reshape(HEAD_PAIRS, 2 * rows, win)
    return pl.pallas_call(
        functools.partial(_band_merge_kernel, rows=rows, win=win, group=group,
                          tiles_per_row_tile=tiles_per_row_tile, row_tiles=row_tiles),
        grid=(steps,),
        in_specs=[q_spec, hist, hist, _resident(bias.shape, (0, 0, 0)), wide, narrow]
        + _merge_weight_specs(),
        out_specs=wide,
        out_shape=jax.ShapeDtypeStruct((b, t, D_MODEL), F32),
        scratch_shapes=([pltpu.VMEM((2, tm, WIDTH), BF16),
                         pltpu.VMEM((tm, D_MODEL), BF16),
                         pltpu.VMEM((tm, 2 * WIDTH + 2 * D_MODEL), F32)]
                        + [pltpu.VMEM((1, 2 * rows, win), F32)] * (group * HEAD_PAIRS)
                        + [pltpu.VMEM((1, 2 * rows, LANES), F32)] * (group * HEAD_PAIRS)),
        compiler_params=pltpu.CompilerParams(
            dimension_semantics=("arbitrary",), vmem_limit_bytes=VMEM_LIMIT),
        name="band_merge",
    )(q, k_hist, v_hist, bias, x, osb, nw, w_bf, w_bf, w_bf, wsb, wcb, wout)


def _heads(a):
    b, t, _ = a.shape
    return a.reshape(b, t, N_HEADS, HEAD_DIM)


def kernel(x_prompt, x_sample, cache_sb_k, cache_sb_v, cache_cb_k, cache_cb_v, norm_w, w_in,
           q_norm_w, k_norm_w, rel_bias, w_proj_sb, w_proj_cb, w_out):
    depth = w_in.shape[0]
    b, t, _ = x_prompt.shape
    bs, tn, _ = x_sample.shape
    p = cache_sb_k.shape[2]
    r = cache_cb_k.shape[2]
    assert t % (2 * CHUNK) == 0 and tn == CHUNK and r == BAND and p % (2 * tn) == 0

    y_p, y_s = x_prompt, x_sample
    outs = [[] for _ in range(8)]
    for l in range(depth):
        nw = norm_w[l].reshape(1, D_MODEL)
        w_bf = w_in[l].astype(BF16)
        qnw = jnp.tile(q_norm_w[l], 2).reshape(1, LANES)
        knw = jnp.tile(k_norm_w[l], 2).reshape(1, LANES)
        rb_pad = jnp.pad(rel_bias[l], ((0, 0), (0, 384 - (2 * REL_CLIP + 1))))
        bias2 = _bias_table(rb_pad, 2)
        bias1 = bias2[:, :CHUNK, :(LEFT_CHUNKS + 1) * CHUNK]
        wsb = w_proj_sb[l].astype(BF16)
        wcb = w_proj_cb[l].astype(BF16)
        wout = w_out[l].astype(BF16)

        (sbq, sbk32, sbk16, sbv32, sbv16, cbq, cbk16, cbv16, cbk32, cbv32) = _project(
            y_p, nw, w_bf, qnw, knw, tm=512, pad_rows=BAND, tail=min(BAND, t))
        o_sb = _stick_breaking(sbq, sbk16, sbv16, blk=128)
        y_p_next = _band_merge(cbq, cbk16, cbv16, bias2, y_p, o_sb, nw, w_bf, wsb, wcb, wout,
                               g=2, group=2, tm=512)
        for lst, a in zip(outs[:4], (sbk32, sbv32, cbk32, cbv32)):
            lst.append(_heads(a))

        proj = _project(y_s.reshape(1, bs * tn, D_MODEL), nw, w_bf, qnw, knw,
                        tm=bs * tn, pad_rows=0, tail=bs * tn)
        (sbq, sbk32, sbk16, sbv32, sbv16, cbq, cbk16, cbv16, cbk32, cbv32) = [
            a.reshape(bs, tn, a.shape[-1]) for a in proj]
        o_sb = _stick_breaking(sbq, sbk16, sbv16,
                               jnp.transpose(cache_sb_k[l], (0, 2, 3, 1)),
                               jnp.transpose(cache_sb_v[l], (0, 2, 3, 1)), blk=tn)
        k_hist = jnp.concatenate([cache_cb_k[l].reshape(bs, r, WIDTH).astype(BF16), cbk16], axis=1)
        v_hist = jnp.concatenate([cache_cb_v[l].reshape(bs, r, WIDTH).astype(BF16), cbv16], axis=1)
        o_cb = _chunk_band(cbq, k_hist, v_hist, bias1, g=1, tiles=1, pad_history=False)
        flat = lambda a: a.reshape(1, bs * tn, a.shape[-1])
        y_s_next = _merge(flat(y_s), flat(o_sb), flat(o_cb), nw, w_bf, wsb, wcb, wout,
                          tm=bs * tn).reshape(bs, tn, D_MODEL)
        for lst, a in zip(outs[4:], (sbk32, sbv32, cbk32, cbv32)):
            lst.append(_heads(a))

        y_p, y_s = y_p_next, y_s_next

    return (y_p, y_s) + tuple(jnp.stack(o) for o in outs)
```

```python
import functools

import jax
import jax.numpy as jnp
import numpy as np
from jax import lax
from jax.experimental import pallas as pl
from jax.experimental.pallas import tpu as pltpu

F32 = jnp.float32
BF16 = jnp.bfloat16

D_MODEL = 1024
CHUNK = 64
LEFT_CHUNKS = 8
BAND = LEFT_CHUNKS * CHUNK
N_HEADS = 8
HEAD_DIM = 64
WIDTH = N_HEADS * HEAD_DIM
REL_CLIP = 128
EPS = 1e-6
NEG_INF = -1e30
ATTN_SCALE = 0.125
IN_COLS = 4 * WIDTH + 4 * WIDTH + 2 * D_MODEL

LANES = 128
HEAD_PAIRS = N_HEADS // 2
EXP_ZERO_BELOW = -104.0
LOG2E = 1.4426950408889634
VMEM_LIMIT = 56 * 1024 * 1024


def _dot(a, b):
    return jnp.dot(a, b, preferred_element_type=F32)


def _dot_nt(a, b):
    return lax.dot_general(a, b, (((1,), (1,)), ((), ())), preferred_element_type=F32)


def _sigmoid(x):
    return 1.0 / (1.0 + jnp.exp(-x))


def _head_rms_norm(h, w128, lane_lo):
    sq = h * h
    s_lo = jnp.sum(jnp.where(lane_lo, sq, 0.0), axis=-1, keepdims=True)
    s_hi = jnp.sum(jnp.where(lane_lo, 0.0, sq), axis=-1, keepdims=True)
    r_lo = lax.rsqrt(s_lo * (1.0 / HEAD_DIM) + EPS)
    r_hi = lax.rsqrt(s_hi * (1.0 / HEAD_DIM) + EPS)
    return (h * jnp.where(lane_lo, r_lo, r_hi)) * w128


def _normed_bf16(x, nw):
    ms = jnp.mean(x * x, axis=-1, keepdims=True)
    return ((x * lax.rsqrt(ms + EPS)) * nw).astype(BF16)


def _proj_kernel(x_ref, nw_ref, w_ref, qnw_ref, knw_ref,
                 sbq_ref, sbk32_ref, sbk16_ref, sbv32_ref, sbv16_ref,
                 cbq_ref, cbk16_ref, cbv16_ref, cbk32_ref, cbv32_ref, *, pad_blocks):
    j = pl.program_id(1)

    if pad_blocks:
        @pl.when(j < pad_blocks)
        def _():
            for ref in (sbk16_ref, sbv16_ref, cbk16_ref, cbv16_ref):
                ref[...] = jnp.zeros(ref.shape, BF16)

    @pl.when(j >= pad_blocks)
    def _():
        xb = _normed_bf16(x_ref[0], nw_ref[...])
        tm = xb.shape[0]
        lane_lo = lax.broadcasted_iota(jnp.int32, (tm, LANES), 1) < HEAD_DIM

        def col(c0, width):
            return _dot(xb, w_ref[:, c0:c0 + width])

        sbq_ref[0] = (col(0, WIDTH) * ATTN_SCALE).astype(BF16)
        h = col(WIDTH, WIDTH)
        sbk32_ref[0] = h
        sbk16_ref[0] = h.astype(BF16)
        h = col(2 * WIDTH, WIDTH)
        sbv32_ref[0] = h
        sbv16_ref[0] = h.astype(BF16)

        hq_all = col(4 * WIDTH, WIDTH)
        hk_all = col(5 * WIDTH, WIDTH)
        for s in range(HEAD_PAIRS):
            sl = slice(s * LANES, (s + 1) * LANES)
            hq = _head_rms_norm(hq_all[:, sl], qnw_ref[...], lane_lo)
            cbq_ref[0, :, sl] = (hq * (ATTN_SCALE * LOG2E)).astype(BF16)
            hk = _head_rms_norm(hk_all[:, sl], knw_ref[...], lane_lo)
            cbk16_ref[0, :, sl] = hk.astype(BF16)
            cbk32_ref[0, :, sl] = hk

        h = col(6 * WIDTH, WIDTH)
        cbv16_ref[0] = h.astype(BF16)
        cbv32_ref[0] = h


def _project(x, nw, w_bf, qnw128, knw128, *, tm, pad_rows, tail):
    b, t, _ = x.shape
    nt = t // tm
    pad_blocks = pad_rows // tm
    tail_blocks = tail // tm
    first_tail_step = pad_blocks + nt - tail_blocks

    def row(bi, j):
        return (bi, jnp.maximum(j - pad_blocks, 0), 0)

    def padded(bi, j):
        return (bi, j, 0)

    def tail_map(bi, j):
        return (bi, jnp.maximum(j - first_tail_step, 0), 0)

    def const(bi, j):
        return (0, 0)

    def out(width, dtype, rows=t):
        return jax.ShapeDtypeStruct((b, rows, width), dtype)

    blk = lambda width, imap: pl.BlockSpec((1, tm, width), imap)
    out_shape = [
        out(WIDTH, BF16), out(WIDTH, F32), out(WIDTH, BF16, t + pad_rows),
        out(WIDTH, F32), out(WIDTH, BF16, t + pad_rows),
        out(WIDTH, BF16), out(WIDTH, BF16, t + pad_rows), out(WIDTH, BF16, t + pad_rows),
        out(WIDTH, F32, tail), out(WIDTH, F32, tail),
    ]
    out_specs = [
        blk(WIDTH, row), blk(WIDTH, row), blk(WIDTH, padded),
        blk(WIDTH, row), blk(WIDTH, padded),
        blk(WIDTH, row), blk(WIDTH, padded), blk(WIDTH, padded),
        blk(WIDTH, tail_map), blk(WIDTH, tail_map),
    ]
    in_specs = [
        blk(D_MODEL, row),
        pl.BlockSpec((1, D_MODEL), const),
        pl.BlockSpec((D_MODEL, IN_COLS), const, pipeline_mode=pl.Buffered(1)),
        pl.BlockSpec((1, LANES), const),
        pl.BlockSpec((1, LANES), const),
    ]
    return pl.pallas_call(
        functools.partial(_proj_kernel, pad_blocks=pad_blocks),
        grid=(b, nt + pad_blocks),
        in_specs=in_specs, out_specs=out_specs, out_shape=out_shape,
        compiler_params=pltpu.CompilerParams(
            dimension_semantics=("arbitrary", "arbitrary"), vmem_limit_bytes=VMEM_LIMIT),
        name="project",
    )(x, nw, w_bf, qnw128, knw128)


def _sb_visit_items(scratch, uo, blk, jobs, n, causal):
    qm_all, acc_all, carry_all, z_all, sc_all = scratch
    sub = lambda u: slice(u * blk, (u + 1) * blk)
    masked = lambda u: causal is not None and u == n - 1

    def stage1(t, scores, h):
        z_all[t, h, :, :n * blk] = scores(h, qm_all[t, h])

    def stage2(t, h):
        zh = z_all[t, h, :, :n * blk]
        sp = jnp.maximum(zh, 0.0) + jnp.log(1.0 + jnp.exp2(jnp.abs(zh) * (-LOG2E)))
        for u in range(n):
            spu = jnp.where(causal, sp[:, sub(u)], 0.0) if masked(u) else sp[:, sub(u)]
            hi = spu.astype(BF16)
            lo = (spu - hi.astype(F32)).astype(BF16)
            sc_all[t, h, u] = _dot(jnp.concatenate([hi, lo], axis=1), uo)

    def stage3(t, weighted, h):
        carry = carry_all[t, h]
        ws = [None] * n
        for u in reversed(range(n)):
            w = jnp.exp(z_all[t, h, :, sub(u)] - sc_all[t, h, u, :, :blk] - carry)
            if masked(u):
                w = jnp.where(causal, w, 0.0)
            ws[u] = w.astype(BF16)
            carry = carry + sc_all[t, h, u, :, blk:]
        carry_all[t, h] = carry
        acc_all[t, h] += weighted(h, jnp.concatenate(ws, axis=1))

    heads = range(N_HEADS)
    return ([functools.partial(stage1, t, sc, h) for t, sc, _ in jobs for h in heads]
            + [functools.partial(stage2, t, h) for t, _, _ in jobs for h in heads]
            + [functools.partial(stage3, t, wt, h) for t, _, wt in jobs for h in heads])


def _sb_live(scratch, t):
    return jnp.min(scratch[2][t]) < -EXP_ZERO_BELOW


def _interleave(main, extra):
    if not extra:
        return list(main)
    every = max(len(main) // len(extra), 1)
    out, k = [], 0
    for n, item in enumerate(main):
        if n % every == 0 and k < len(extra):
            out.append(extra[k])
            k += 1
        out.append(item)
    return out + list(extra[k:])


def _sb_prompt_blocks(q_rows, k_rows, v_rows, store, scratch, uo, blk, blocks, new_pad,
                      extra=()):
    qm_all, acc_all, carry_all = scratch[:3]
    tiles = range(len(blocks))
    lane_lo = lax.broadcasted_iota(jnp.int32, (blk, LANES), 1) < HEAD_DIM
    causal = (lax.broadcasted_iota(jnp.int32, (blk, blk), 1)
              < lax.broadcasted_iota(jnp.int32, (blk, blk), 0))
    pair = lambda h: slice((h // 2) * LANES, (h // 2 + 1) * LANES)
    diags = [pl.multiple_of(new_pad + ib * blk, blk) for ib in blocks]
    for t in tiles:
        acc_all[t] = jnp.zeros(acc_all.shape[1:], F32)
        carry_all[t] = jnp.zeros(carry_all.shape[1:], F32)
        q = q_rows(t)
        for hp in range(HEAD_PAIRS):
            q2 = q[:, pair(2 * hp)]
            zero = jnp.zeros_like(q2)
            qm_all[t, 2 * hp] = jnp.where(lane_lo, q2, zero)
            qm_all[t, 2 * hp + 1] = jnp.where(lane_lo, zero, q2)

    def new_rows(t, s, n):
        k_tile = k_rows(s, n)
        v_tile = v_rows(s, n)
        return (t, lambda h, q: _dot_nt(q, k_tile[:, pair(h)]),
                lambda h, w: _dot(w, v_tile[:, pair(h)]))

    first = []
    for t in tiles:
        first += _sb_visit_items(
            scratch, uo, blk, [new_rows(t, pl.multiple_of(diags[t] - 2 * blk, blk), 3)], 3,
            causal)
    for item in _interleave(first, list(extra)):
        item()

    for t in tiles:
        def older_new(c, t=t):
            j, _ = c
            s = pl.multiple_of(diags[t] - (j + 1) * 2 * blk, blk)
            for item in _sb_visit_items(scratch, uo, blk, [new_rows(t, s, 2)], 2, None):
                item()
            return j + 1, _sb_live(scratch, t)

        def cond_new(c, t=t):
            j, live = c
            return jnp.logical_and(j < (blocks[t] + 1) // 2, live)
        lax.while_loop(cond_new, older_new, (jnp.int32(1), _sb_live(scratch, t)))

    for t in tiles:
        for hp in range(HEAD_PAIRS):
            store(t, hp, jnp.where(lane_lo, acc_all[t, 2 * hp], acc_all[t, 2 * hp + 1]))


def _sb_kernel(*refs, blk, qtiles, new_pad, n_cache):
    if n_cache:
        q_ref, kn_ref, vn_ref, kc_ref, vc_ref, uo_ref, o_ref = refs[:7]
    else:
        q_ref, kn_ref, vn_ref, uo_ref, o_ref = refs[:5]
    scratch = refs[-5:]
    qm_all, acc_all, carry_all = scratch[:3]
    uo = uo_ref[...]

    if new_pad is not None:
        rows_of = lambda t: slice(t * blk, (t + 1) * blk)

        def store(t, hp, value):
            o_ref[0, rows_of(t), hp * LANES:(hp + 1) * LANES] = value.astype(o_ref.dtype)

        _sb_prompt_blocks(lambda t: q_ref[0, rows_of(t), :],
                          lambda s, n: kn_ref[0, pl.ds(s, n * blk), :],
                          lambda s, n: vn_ref[0, pl.ds(s, n * blk), :],
                          store, scratch, uo, blk,
                          [pl.program_id(1) * qtiles + t for t in range(qtiles)], new_pad)
    else:
        head = lambda h: slice(h * HEAD_DIM, (h + 1) * HEAD_DIM)
        causal = (lax.broadcasted_iota(jnp.int32, (blk, blk), 1)
                  < lax.broadcasted_iota(jnp.int32, (blk, blk), 0))
        acc_all[...] = jnp.zeros(acc_all.shape, F32)
        carry_all[...] = jnp.zeros(carry_all.shape, F32)
        for h in range(N_HEADS):
            qm_all[0, h] = q_ref[0, :, head(h)]

        def visit(job, n, causal):
            for item in _sb_visit_items(scratch, uo, blk, [job], n, causal):
                item()
            return _sb_live(scratch, 0)

        def cache_cols(ref, h, s):
            return ref[0, h, :, pl.ds(s, 2 * blk)].astype(BF16)

        def cache_rows(s):
            return (lambda h, q: _dot(q, cache_cols(kc_ref, h, s)),
                    lambda h, w: _dot_nt(w, cache_cols(vc_ref, h, s)))

        top = (n_cache - 1) * 2 * blk
        old_scores, old_weighted = cache_rows(top)
        live = visit(
            (0,
             lambda h, q: jnp.concatenate(
                 [old_scores(h, q), _dot_nt(q, kn_ref[0, :, head(h)])], axis=1),
             lambda h, w: (old_weighted(h, w[:, :2 * blk])
                           + _dot(w[:, 2 * blk:], vn_ref[0, :, head(h)]))),
            3, causal)

        def older_cache(c):
            j, _ = c
            s = pl.multiple_of(j * 2 * blk, 2 * blk)
            return j - 1, visit((0,) + cache_rows(s), 2, None)

        def cond_cache(c):
            j, live = c
            return jnp.logical_and(j >= 0, live)
        lax.while_loop(cond_cache, older_cache, (jnp.int32(n_cache - 2), live))

        for h in range(N_HEADS):
            o_ref[0, :, head(h)] = acc_all[0, h].astype(o_ref.dtype)


def _proj_sb_kernel(x_ref, nw_ref, wsb_ref, wcbqk_ref, wcbv_ref, qnw_ref, knw_ref, uo_ref,
                    sbk32_ref, sbv32_ref, cbq_ref, cbk16_ref, cbv16_ref, cbk32_ref, cbv32_ref,
                    osb_ref, xb_ref, q_s, k_s, v_s, qt_s, kt_s, vt_s, *scratch,
                    blk, qtiles, tm, row_tiles, pad):
    s = pl.program_id(0)
    per_seq = row_tiles + 1
    j = s % per_seq
    uo = uo_ref[...]
    lane_lo = lax.broadcasted_iota(jnp.int32, (tm, LANES), 1) < HEAD_DIM

    def zero_pad_blocks():
        cbk16_ref[...] = jnp.zeros(cbk16_ref.shape, BF16)
        cbv16_ref[...] = jnp.zeros(cbv16_ref.shape, BF16)

    def project_items():
        def normalise():
            xb_ref[...] = _normed_bf16(x_ref[0], nw_ref[...])

        def col(w_ref, c0):
            return _dot(xb_ref[...], w_ref[:, c0:c0 + WIDTH])

        def sb_q():
            qt_s[...] = (col(wsb_ref, 0) * ATTN_SCALE).astype(BF16)

        def sb_k():
            h = col(wsb_ref, WIDTH)
            sbk32_ref[0] = h
            kt_s[...] = h.astype(BF16)

        def sb_v():
            h = col(wsb_ref, 2 * WIDTH)
            sbv32_ref[0] = h
            vt_s[...] = h.astype(BF16)

        def cb_q():
            h = col(wcbqk_ref, 0)
            for p in range(HEAD_PAIRS):
                sl = slice(p * LANES, (p + 1) * LANES)
                hq = _head_rms_norm(h[:, sl], qnw_ref[...], lane_lo)
                cbq_ref[0, :, sl] = (hq * (ATTN_SCALE * LOG2E)).astype(BF16)

        def cb_k():
            h = col(wcbqk_ref, WIDTH)
            for p in range(HEAD_PAIRS):
                sl = slice(p * LANES, (p + 1) * LANES)
                hk = _head_rms_norm(h[:, sl], knw_ref[...], lane_lo)
                cbk16_ref[0, :, sl] = hk.astype(BF16)
                cbk32_ref[0, :, sl] = hk

        def cb_v():
            h = col(wcbv_ref, 0)
            cbv16_ref[0] = h.astype(BF16)
            cbv32_ref[0] = h

        return [normalise, sb_q, sb_k, sb_v], [cb_q, cb_k, cb_v]

    def attend(prev_tile, extras):
        rows = pl.ds(pl.multiple_of(pad + prev_tile * tm, tm), tm)
        q_s[...] = qt_s[...]
        k_s[rows, :] = kt_s[...]
        v_s[rows, :] = vt_s[...]
        for g in range(tm // (blk * qtiles)):
            rows_of = lambda t, g=g: slice((g * qtiles + t) * blk, (g * qtiles + t + 1) * blk)

            def store(t, hp, value, rows_of=rows_of):
                osb_ref[0, rows_of(t), hp * LANES:(hp + 1) * LANES] = value.astype(osb_ref.dtype)

            _sb_prompt_blocks(
                lambda t, rows_of=rows_of: q_s[rows_of(t), :],
                lambda r, n: k_s[pl.ds(r, n * blk), :],
                lambda r, n: v_s[pl.ds(r, n * blk), :],
                store, scratch, uo, blk,
                [prev_tile * (tm // blk) + g * qtiles + t for t in range(qtiles)], pad,
                extra=extras[g] if g < len(extras) else ())
        for late in extras[tm // (blk * qtiles):]:
            for item in late:
                item()

    @pl.when(s == 0)
    def _():
        k_s[0:pad, :] = jnp.zeros((pad, WIDTH), BF16)
        v_s[0:pad, :] = jnp.zeros((pad, WIDTH), BF16)

    @pl.when(jnp.logical_and(j == 0, s < pl.num_programs(0) - 1))
    def _():
        zero_pad_blocks()

    @pl.when(jnp.logical_and(j == 0, s > 0))
    def _():
        attend(row_tiles - 1, [])

    @pl.when(j == 1)
    def _():
        first, second = project_items()
        for item in first + second:
            item()

    @pl.when(j >= 2)
    def _():
        attend(j - 2, list(project_items()))


def _project_sb(x, nw, w_bf, qnw128, knw128, *, tm, blk, pad):
    b, t, _ = x.shape
    n = t // tm
    per_seq = n + 1
    steps = b * per_seq + 1
    qtiles = 2
    tail = min(BAND, t)
    assert t % tm == 0 and pad == tm and tail == tm and tm % (blk * qtiles) == 0

    def last_projected(s):
        g = jnp.maximum(s - s // per_seq - 1, 0)
        return g // n, g % n

    def row(s):
        return last_projected(s) + (0,)

    def padded(s):
        c = jnp.minimum(s, steps - 2)
        return (c // per_seq, c % per_seq, 0)

    def tail_map(s):
        return (last_projected(s)[0], 0, 0)

    def attended(s):
        return last_projected(jnp.maximum(s - 1, 0)) + (0,)

    blkspec = lambda width, imap: pl.BlockSpec((1, tm, width), imap)
    uo = _cumsum_matrix(blk)
    out = lambda width, dtype, rows=t: jax.ShapeDtypeStruct((b, rows, width), dtype)
    out_shape = [out(WIDTH, F32), out(WIDTH, F32), out(WIDTH, BF16),
                 out(WIDTH, BF16, t + pad), out(WIDTH, BF16, t + pad),
                 out(WIDTH, F32, tail), out(WIDTH, F32, tail), out(WIDTH, BF16)]
    out_specs = [blkspec(WIDTH, row), blkspec(WIDTH, row), blkspec(WIDTH, row),
                 blkspec(WIDTH, padded), blkspec(WIDTH, padded),
                 blkspec(WIDTH, tail_map), blkspec(WIDTH, tail_map), blkspec(WIDTH, attended)]
    in_specs = [blkspec(D_MODEL, row),
                _resident((1, D_MODEL), (0, 0)),
                _resident((D_MODEL, 3 * WIDTH), (0, 0)),
                _resident((D_MODEL, 2 * WIDTH), (0, 2)),
                _resident((D_MODEL, WIDTH), (0, 6)),
                _resident((1, LANES), (0, 0)), _resident((1, LANES), (0, 0)),
                _resident(uo.shape, (0, 0))]
    d = LANES
    return pl.pallas_call(
        functools.partial(_proj_sb_kernel, blk=blk, qtiles=qtiles, tm=tm, row_tiles=n, pad=pad),
        grid=(steps,),
        in_specs=in_specs, out_specs=out_specs, out_shape=out_shape,
        scratch_shapes=[pltpu.VMEM((tm, D_MODEL), BF16),
                        pltpu.VMEM((tm, WIDTH), BF16),
                        pltpu.VMEM((pad + t, WIDTH), BF16),
                        pltpu.VMEM((pad + t, WIDTH), BF16),
                        pltpu.VMEM((tm, WIDTH), BF16),
                        pltpu.VMEM((tm, WIDTH), BF16),
                        pltpu.VMEM((tm, WIDTH), BF16),
                        pltpu.VMEM((qtiles, N_HEADS, blk, d), BF16),
                        pltpu.VMEM((qtiles, N_HEADS, blk, d), F32),
                        pltpu.VMEM((qtiles, N_HEADS, blk, blk), F32),
                        pltpu.VMEM((qtiles, N_HEADS, blk, 3 * blk), F32),
                        pltpu.VMEM((qtiles, N_HEADS, 3, blk, 2 * blk), F32)],
        compiler_params=pltpu.CompilerParams(
            dimension_semantics=("arbitrary",), vmem_limit_bytes=VMEM_LIMIT),
        name="project_sb",
    )(x, nw, w_bf, w_bf, w_bf, qnw128, knw128, uo)


def _cumsum_matrix(blk):
    j = np.arange(blk)[:, None]
    s = np.arange(blk)[None, :]
    u = (j >= s).astype(np.float32)
    half = np.concatenate([u, np.ones((blk, blk), np.float32)], axis=1)
    return jnp.asarray(np.concatenate([half, half], axis=0), dtype=BF16)


def _stick_breaking(q, k_new, v_new, k_cache=None, v_cache=None, *, blk):
    b, tq, _ = q.shape
    nq = tq // blk
    new_pad = k_new.shape[1] - tq if nq > 1 else None
    assert (new_pad is None) == (k_cache is not None)
    assert new_pad is None or new_pad >= 2 * blk
    n_cache = 0 if k_cache is None else k_cache.shape[-1] // (2 * blk)
    d = HEAD_DIM if n_cache else LANES
    full = lambda arr: pl.BlockSpec((1,) + arr.shape[1:],
                                    lambda bi, i: (bi,) + (0,) * (arr.ndim - 1))
    qtiles = 2 if nq % 2 == 0 else 1
    qblk = pl.BlockSpec((1, qtiles * blk, WIDTH), lambda bi, i: (bi, i, 0))
    uo = _cumsum_matrix(blk)
    operands = [q, k_new, v_new]
    in_specs = [qblk, full(k_new), full(v_new)]
    if n_cache:
        operands += [k_cache, v_cache]
        in_specs += [full(k_cache), full(v_cache)]
    operands.append(uo)
    in_specs.append(pl.BlockSpec(uo.shape, lambda bi, i: (0, 0)))
    return pl.pallas_call(
        functools.partial(_sb_kernel, blk=blk, qtiles=qtiles, new_pad=new_pad, n_cache=n_cache),
        grid=(b, nq // qtiles),
        in_specs=in_specs, out_specs=qblk,
        out_shape=jax.ShapeDtypeStruct((b, tq, WIDTH), BF16),
        scratch_shapes=[pltpu.VMEM((qtiles, N_HEADS, blk, d), BF16),
                        pltpu.VMEM((qtiles, N_HEADS, blk, d), F32),
                        pltpu.VMEM((qtiles, N_HEADS, blk, blk), F32),
                        pltpu.VMEM((qtiles, N_HEADS, blk, 3 * blk), F32),
                        pltpu.VMEM((qtiles, N_HEADS, 3, blk, 2 * blk), F32)],
        compiler_params=pltpu.CompilerParams(
            dimension_semantics=("arbitrary", "arbitrary"), vmem_limit_bytes=VMEM_LIMIT),
        name="stick_breaking",
    )(*operands)


def _bias_kernel(rb_ref, o_ref, *, g):
    rows, win = g * CHUNK, (LEFT_CHUNKS + g) * CHUNK
    n = 768
    npad = rb_ref.shape[1]
    jj = lax.broadcasted_iota(jnp.int32, (npad, n), 1)
    rr = lax.broadcasted_iota(jnp.int32, (npad, n), 0)
    idx = jnp.where(jj > win, 2 * REL_CLIP,
                    jnp.clip(BAND - jj, -REL_CLIP, REL_CLIP) + REL_CLIP)
    sel = (rr == idx).astype(BF16)
    rb = rb_ref[...]
    hi = rb.astype(BF16)
    r1 = rb - hi.astype(F32)
    mid = r1.astype(BF16)
    lo = (r1 - mid.astype(F32)).astype(BF16)
    line = _dot(hi, sel) + _dot(mid, sel) + _dot(lo, sel)
    line = line * LOG2E
    tq = lax.broadcasted_iota(jnp.int32, (rows, win), 0)
    tk = lax.broadcasted_iota(jnp.int32, (rows, win), 1)
    off = tk - (tq // CHUNK) * CHUNK
    valid = jnp.logical_and(off >= 0, off < (LEFT_CHUNKS + 1) * CHUNK)
    for h in range(N_HEADS):
        tiled = jnp.broadcast_to(line[h:h + 1, :], (rows, n))
        rolled = pltpu.roll(tiled, 0, 1, stride=1, stride_axis=0)
        o_ref[h] = jnp.where(valid, rolled[:, :win], NEG_INF)


def _bias_table(rb_pad, g):
    rows, win = g * CHUNK, (LEFT_CHUNKS + g) * CHUNK
    return pl.pallas_call(
        functools.partial(_bias_kernel, g=g),
        out_shape=jax.ShapeDtypeStruct((N_HEADS, rows, win), F32),
        name="bias_table",
    )(rb_pad)


def _band_tiles(q_ref, k_ref, v_ref, bias_ref, store, s_refs, m_refs, zero_idx, *,
                rows, win, tile_rows, tile_starts, mask_pad):
    lane_lo = lax.broadcasted_iota(jnp.int32, (rows, LANES), 1) < HEAD_DIM
    work = [(t, hp) for t in range(len(tile_rows)) for hp in range(HEAD_PAIRS)]

    def first(t, hp):
        sl = slice(hp * LANES, (hp + 1) * LANES)
        q2 = q_ref[0, tile_rows[t], sl]
        zero = jnp.zeros_like(q2)
        qm = jnp.concatenate([jnp.where(lane_lo, q2, zero), jnp.where(lane_lo, zero, q2)],
                             axis=0)
        s = _dot_nt(qm, k_ref[0, pl.ds(tile_starts[t], win), sl]) + bias_ref[hp]
        if mask_pad:
            key_ok = (lax.broadcasted_iota(jnp.int32, (2 * rows, win), 1)
                      >= BAND - tile_starts[t])
            s = jnp.where(key_ok, s, NEG_INF)
        s_refs[t * HEAD_PAIRS + hp][0] = s
        m_refs[t * HEAD_PAIRS + hp][0] = jnp.broadcast_to(
            jnp.max(s, axis=-1, keepdims=True), (2 * rows, LANES))

    def second(t, hp):
        sl = slice(hp * LANES, (hp + 1) * LANES)
        s = s_refs[t * HEAD_PAIRS + hp][zero_idx]
        m = m_refs[t * HEAD_PAIRS + hp][zero_idx]
        m = jnp.concatenate([m] * (win // LANES), axis=1) if win % LANES == 0 else m[:, :1]
        e = jnp.exp2(s - m)
        l = jnp.sum(e, axis=-1, keepdims=True)
        o = _dot(e.astype(BF16), v_ref[0, pl.ds(tile_starts[t], win), sl]) / l
        store(t, hp, jnp.where(lane_lo, o[:rows], o[rows:]))

    return ([functools.partial(first, t, hp) for t, hp in work]
            + [functools.partial(second, t, hp) for t, hp in work])


def _cb_kernel(q_ref, k_ref, v_ref, bias_ref, o_ref, *scratch, rows, win, tiles, pad_history):
    n_slots = tiles * HEAD_PAIRS
    s_refs, m_refs = scratch[:n_slots], scratch[n_slots:]
    p = pl.program_id(1)
    tile_rows = [slice(t * rows, (t + 1) * rows) for t in range(tiles)]

    def store(t, hp, value):
        o_ref[0, tile_rows[t], hp * LANES:(hp + 1) * LANES] = value.astype(o_ref.dtype)

    def body(mask_pad):
        starts = [pl.multiple_of((p * tiles + t) * rows, rows) for t in range(tiles)]
        for item in _band_tiles(q_ref, k_ref, v_ref, bias_ref, store, s_refs, m_refs,
                                jnp.minimum(p, 0), rows=rows, win=win, tile_rows=tile_rows,
                                tile_starts=starts, mask_pad=mask_pad):
            item()

    if pad_history:
        pad_steps = BAND // (rows * tiles)
        pl.when(p < pad_steps)(functools.partial(body, True))
        pl.when(p >= pad_steps)(functools.partial(body, False))
    else:
        body(False)


def _chunk_band(q, k_hist, v_hist, bias, *, g, tiles, pad_history):
    b, t, _ = q.shape
    rows, win = g * CHUNK, (LEFT_CHUNKS + g) * CHUNK
    assert BAND % (rows * tiles) == 0 and t % (rows * tiles) == 0
    full = lambda arr: pl.BlockSpec((1,) + arr.shape[1:], lambda bi, i: (bi, 0, 0))
    qblk = pl.BlockSpec((1, tiles * rows, WIDTH), lambda bi, i: (bi, i, 0))
    bias = bias.reshape(HEAD_PAIRS, 2 * rows, win)
    return pl.pallas_call(
        functools.partial(_cb_kernel, rows=rows, win=win, tiles=tiles, pad_history=pad_history),
        grid=(b, t // (rows * tiles)),
        in_specs=[qblk, full(k_hist), full(v_hist),
                  pl.BlockSpec(bias.shape, lambda bi, i: (0, 0, 0))],
        out_specs=qblk,
        out_shape=jax.ShapeDtypeStruct((b, t, WIDTH), BF16),
        scratch_shapes=([pltpu.VMEM((1, 2 * rows, win), F32)] * (tiles * HEAD_PAIRS)
                        + [pltpu.VMEM((1, 2 * rows, LANES), F32)] * (tiles * HEAD_PAIRS)),
        compiler_params=pltpu.CompilerParams(
            dimension_semantics=("arbitrary", "arbitrary"), vmem_limit_bytes=VMEM_LIMIT),
        name="chunk_band",
    )(q, k_hist, v_hist, bias)


def _merge_rows(x, osb, ocb, nw, wzsb, wzcb, wg, wsb, wcb, wout):
    xb = _normed_bf16(x, nw)
    zsb = _dot(xb, wzsb)
    zcb = _dot(xb, wzcb)
    a_sb = (osb.astype(F32) * (zsb * _sigmoid(zsb))).astype(BF16)
    a_cb = (ocb.astype(F32) * (zcb * _sigmoid(zcb))).astype(BF16)
    b_sb = _dot(a_sb, wsb)
    b_cb = _dot(a_cb, wcb)
    g = _dot(xb, wg)
    h = _sigmoid(g[:, :D_MODEL]) * b_sb + _sigmoid(g[:, D_MODEL:]) * b_cb
    return x + _dot(h.astype(BF16), wout)


def _merge_kernel(x_ref, osb_ref, ocb_ref, nw_ref, wzsb_ref, wzcb_ref, wg_ref,
                  wsb_ref, wcb_ref, wout_ref, y_ref):
    y_ref[0] = _merge_rows(x_ref[0], osb_ref[0], ocb_ref[0], nw_ref[...], wzsb_ref[...],
                           wzcb_ref[...], wg_ref[...], wsb_ref[...], wcb_ref[...],
                           wout_ref[...])


def _resident(shape, index):
    return pl.BlockSpec(shape, lambda *_: index, pipeline_mode=pl.Buffered(1))


def _merge_weight_specs():
    return [_resident((1, D_MODEL), (0, 0)),
            _resident((D_MODEL, WIDTH), (0, 3)), _resident((D_MODEL, WIDTH), (0, 7)),
            _resident((D_MODEL, 2 * D_MODEL), (0, 2)),
            _resident((WIDTH, D_MODEL), (0, 0)), _resident((WIDTH, D_MODEL), (0, 0)),
            _resident((D_MODEL, D_MODEL), (0, 0))]


def _merge(x, osb, ocb, nw, w_bf, wsb, wcb, wout, *, tm):
    b, t, _ = x.shape
    wide = pl.BlockSpec((1, tm, D_MODEL), lambda bi, i: (bi, i, 0))
    narrow = pl.BlockSpec((1, tm, WIDTH), lambda bi, i: (bi, i, 0))
    return pl.pallas_call(
        _merge_kernel,
        grid=(b, t // tm),
        in_specs=[wide, narrow, narrow] + _merge_weight_specs(),
        out_specs=wide,
        out_shape=jax.ShapeDtypeStruct((b, t, D_MODEL), F32),
        compiler_params=pltpu.CompilerParams(
            dimension_semantics=("arbitrary", "arbitrary"), vmem_limit_bytes=VMEM_LIMIT),
        name="merge",
    )(x, osb, ocb, nw, w_bf, w_bf, w_bf, wsb, wcb, wout)


def _band_merge_kernel(q_ref, k_ref, v_ref, bias_ref, x_ref, osb_ref, nw_ref, wzsb_ref,
                       wzcb_ref, wg_ref, wsb_ref, wcb_ref, wout_ref, y_ref, ocb_ref, *scratch,
                       rows, win, group, tiles_per_row_tile, row_tiles):
    n_slots = group * HEAD_PAIRS
    xb_ref, zg_ref = scratch[:2]
    s_refs, m_refs = scratch[2:2 + n_slots], scratch[2 + n_slots:]
    s = pl.program_id(0)
    slot = s % 2
    i = jnp.minimum(s, pl.num_programs(0) - 2) % row_tiles

    @pl.when(s == 0)
    def _():
        ocb_ref[...] = jnp.zeros(ocb_ref.shape, ocb_ref.dtype)

    def normalise():
        xb_ref[...] = _normed_bf16(x_ref[0], nw_ref[...])

    def gate_cols(c0, width, w_ref, w0):
        def run():
            zg_ref[:, c0:c0 + width] = _dot(xb_ref[...], w_ref[:, w0:w0 + width])
        return run

    def finish():
        zsb = zg_ref[:, :WIDTH]
        zcb = zg_ref[:, WIDTH:2 * WIDTH]
        a_sb = (osb_ref[0].astype(F32) * (zsb * _sigmoid(zsb))).astype(BF16)
        a_cb = (ocb_ref[1 - slot].astype(F32) * (zcb * _sigmoid(zcb))).astype(BF16)
        h = (_sigmoid(zg_ref[:, 2 * WIDTH:2 * WIDTH + D_MODEL]) * _dot(a_sb, wsb_ref[...])
             + _sigmoid(zg_ref[:, 2 * WIDTH + D_MODEL:]) * _dot(a_cb, wcb_ref[...]))
        y_ref[0] = x_ref[0] + _dot(h.astype(BF16), wout_ref[...])

    merge_items = [normalise,
                   gate_cols(0, WIDTH, wzsb_ref, 0), gate_cols(WIDTH, WIDTH, wzcb_ref, 0),
                   gate_cols(2 * WIDTH, D_MODEL, wg_ref, 0),
                   gate_cols(2 * WIDTH + D_MODEL, D_MODEL, wg_ref, D_MODEL), finish]

    def body(mask_pad):
        band_items = []
        for g0 in range(0, tiles_per_row_tile, group):
            tile_rows = [slice((g0 + t) * rows, (g0 + t + 1) * rows) for t in range(group)]
            starts = [pl.multiple_of((i * tiles_per_row_tile + g0 + t) * rows, rows)
                      for t in range(group)]

            def store(t, hp, value, tile_rows=tile_rows):
                ocb_ref[slot, tile_rows[t], hp * LANES:(hp + 1) * LANES] = value.astype(
                    ocb_ref.dtype)

            band_items += _band_tiles(q_ref, k_ref, v_ref, bias_ref, store, s_refs, m_refs,
                                      jnp.minimum(s, 0), rows=rows, win=win,
                                      tile_rows=tile_rows, tile_starts=starts,
                                      mask_pad=mask_pad)
        every = len(band_items) // len(merge_items)
        for n, item in enumerate(band_items):
            if n % every == 0 and n // every < len(merge_items):
                merge_items[n // every]()
            item()

    pl.when(i == 0)(functools.partial(body, True))
    pl.when(i != 0)(functools.partial(body, False))


def _band_merge(q, k_hist, v_hist, bias, x, osb, nw, w_bf, wsb, wcb, wout, *, g, group, tm):
    b, t, _ = q.shape
    rows, win = g * CHUNK, (LEFT_CHUNKS + g) * CHUNK
    row_tiles = t // tm
    tiles_per_row_tile = tm // rows
    assert t % tm == 0 and tm % (rows * group) == 0 and BAND <= tm
    steps = b * row_tiles + 1
    last = b * row_tiles - 1

    def cur(s):
        c = jnp.minimum(s, last)
        return c // row_tiles, c % row_tiles

    def prev(s):
        c = jnp.maximum(s - 1, 0)
        return c // row_tiles, c % row_tiles

    q_spec = pl.BlockSpec((1, tm, WIDTH), lambda s: cur(s) + (0,))
    hist = pl.BlockSpec((1,) + k_hist.shape[1:], lambda s: (cur(s)[0], 0, 0),
                        pipeline_mode=pl.Buffered(1))
    wide = pl.BlockSpec((1, tm, D_MODEL), lambda s: prev(s) + (0,))
    narrow = pl.BlockSpec((1, tm, WIDTH), lambda s: prev(s) + (0,))
    bias = bias.reshape(HEAD_PAIRS, 2 * rows, win)
    return pl.pallas_call(
        functools.partial(_band_merge_kernel, rows=rows, win=win, group=group,
                          tiles_per_row_tile=tiles_per_row_tile, row_tiles=row_tiles),
        grid=(steps,),
        in_specs=[q_spec, hist, hist, _resident(bias.shape, (0, 0, 0)), wide, narrow]
        + _merge_weight_specs(),
        out_specs=wide,
        out_shape=jax.ShapeDtypeStruct((b, t, D_MODEL), F32),
        scratch_shapes=([pltpu.VMEM((2, tm, WIDTH), BF16),
                         pltpu.VMEM((tm, D_MODEL), BF16),
                         pltpu.VMEM((tm, 2 * WIDTH + 2 * D_MODEL), F32)]
                        + [pltpu.VMEM((1, 2 * rows, win), F32)] * (group * HEAD_PAIRS)
                        + [pltpu.VMEM((1, 2 * rows, LANES), F32)] * (group * HEAD_PAIRS)),
        compiler_params=pltpu.CompilerParams(
            dimension_semantics=("arbitrary",), vmem_limit_bytes=VMEM_LIMIT),
        name="band_merge",
    )(q, k_hist, v_hist, bias, x, osb, nw, w_bf, w_bf, w_bf, wsb, wcb, wout)


def _heads(a):
    b, t, _ = a.shape
    return a.reshape(b, t, N_HEADS, HEAD_DIM)


def kernel(x_prompt, x_sample, cache_sb_k, cache_sb_v, cache_cb_k, cache_cb_v, norm_w, w_in,
           q_norm_w, k_norm_w, rel_bias, w_proj_sb, w_proj_cb, w_out):
    depth = w_in.shape[0]
    b, t, _ = x_prompt.shape
    bs, tn, _ = x_sample.shape
    p = cache_sb_k.shape[2]
    r = cache_cb_k.shape[2]
    assert t % (2 * CHUNK) == 0 and tn == CHUNK and r == BAND and p % (2 * tn) == 0

    y_p, y_s = x_prompt, x_sample
    outs = [[] for _ in range(8)]
    for l in range(depth):
        nw = norm_w[l].reshape(1, D_MODEL)
        w_bf = w_in[l].astype(BF16)
        qnw = jnp.tile(q_norm_w[l], 2).reshape(1, LANES)
        knw = jnp.tile(k_norm_w[l], 2).reshape(1, LANES)
        rb_pad = jnp.pad(rel_bias[l], ((0, 0), (0, 384 - (2 * REL_CLIP + 1))))
        bias2 = _bias_table(rb_pad, 2)
        bias1 = bias2[:, :CHUNK, :(LEFT_CHUNKS + 1) * CHUNK]
        wsb = w_proj_sb[l].astype(BF16)
        wcb = w_proj_cb[l].astype(BF16)
        wout = w_out[l].astype(BF16)

        (sbk32, sbv32, cbq, cbk16, cbv16, cbk32, cbv32, o_sb) = _project_sb(
            y_p, nw, w_bf, qnw, knw, tm=512, blk=128, pad=BAND)
        y_p_next = _band_merge(cbq, cbk16, cbv16, bias2, y_p, o_sb, nw, w_bf, wsb, wcb, wout,
                               g=2, group=2, tm=512)
        for lst, a in zip(outs[:4], (sbk32, sbv32, cbk32, cbv32)):
            lst.append(_heads(a))

        proj = _project(y_s.reshape(1, bs * tn, D_MODEL), nw, w_bf, qnw, knw,
                        tm=bs * tn, pad_rows=0, tail=bs * tn)
        (sbq, sbk32, sbk16, sbv32, sbv16, cbq, cbk16, cbv16, cbk32, cbv32) = [
            a.reshape(bs, tn, a.shape[-1]) for a in proj]
        o_sb = _stick_breaking(sbq, sbk16, sbv16,
                               jnp.transpose(cache_sb_k[l], (0, 2, 3, 1)),
                               jnp.transpose(cache_sb_v[l], (0, 2, 3, 1)), blk=tn)
        k_hist = jnp.concatenate([cache_cb_k[l].reshape(bs, r, WIDTH).astype(BF16), cbk16], axis=1)
        v_hist = jnp.concatenate([cache_cb_v[l].reshape(bs, r, WIDTH).astype(BF16), cbv16], axis=1)
        o_cb = _chunk_band(cbq, k_hist, v_hist, bias1, g=1, tiles=1, pad_history=False)
        flat = lambda a: a.reshape(1, bs * tn, a.shape[-1])
        y_s_next = _merge(flat(y_s), flat(o_sb), flat(o_cb), nw, w_bf, wsb, wcb, wout,
                          tm=bs * tn).reshape(bs, tn, D_MODEL)
        for lst, a in zip(outs[4:], (sbk32, sbv32, cbk32, cbv32)):
            lst.append(_heads(a))

        y_p, y_s = y_p_next, y_s_next

    return (y_p, y_s) + tuple(jnp.stack(o) for o in outs)
```

```python
import functools

import jax
import jax.numpy as jnp
import numpy as np
from jax import lax
from jax.experimental import pallas as pl
from jax.experimental.pallas import tpu as pltpu

F32 = jnp.float32
BF16 = jnp.bfloat16

D_MODEL = 1024
CHUNK = 64
LEFT_CHUNKS = 8
BAND = LEFT_CHUNKS * CHUNK
N_HEADS = 8
HEAD_DIM = 64
WIDTH = N_HEADS * HEAD_DIM
REL_CLIP = 128
EPS = 1e-6
NEG_INF = -1e30
ATTN_SCALE = 0.125
IN_COLS = 4 * WIDTH + 4 * WIDTH + 2 * D_MODEL

LANES = 128
HEAD_PAIRS = N_HEADS // 2
EXP_ZERO_BELOW = -104.0
LOG2E = 1.4426950408889634
VMEM_LIMIT = 56 * 1024 * 1024


def _dot(a, b):
    return jnp.dot(a, b, preferred_element_type=F32)


def _dot_nt(a, b):
    return lax.dot_general(a, b, (((1,), (1,)), ((), ())), preferred_element_type=F32)


def _sigmoid(x):
    return 1.0 / (1.0 + jnp.exp(-x))


def _head_rms_norm(h, w128, lane_lo):
    sq = h * h
    s_lo = jnp.sum(jnp.where(lane_lo, sq, 0.0), axis=-1, keepdims=True)
    s_hi = jnp.sum(jnp.where(lane_lo, 0.0, sq), axis=-1, keepdims=True)
    r_lo = lax.rsqrt(s_lo * (1.0 / HEAD_DIM) + EPS)
    r_hi = lax.rsqrt(s_hi * (1.0 / HEAD_DIM) + EPS)
    return (h * jnp.where(lane_lo, r_lo, r_hi)) * w128


def _normed_bf16(x, nw):
    ms = jnp.mean(x * x, axis=-1, keepdims=True)
    return ((x * lax.rsqrt(ms + EPS)) * nw).astype(BF16)


def _proj_kernel(x_ref, nw_ref, w_ref, qnw_ref, knw_ref,
                 sbq_ref, sbk32_ref, sbk16_ref, sbv32_ref, sbv16_ref,
                 cbq_ref, cbk16_ref, cbv16_ref, cbk32_ref, cbv32_ref, *, pad_blocks):
    j = pl.program_id(1)

    if pad_blocks:
        @pl.when(j < pad_blocks)
        def _():
            for ref in (sbk16_ref, sbv16_ref, cbk16_ref, cbv16_ref):
                ref[...] = jnp.zeros(ref.shape, BF16)

    @pl.when(j >= pad_blocks)
    def _():
        xb = _normed_bf16(x_ref[0], nw_ref[...])
        tm = xb.shape[0]
        lane_lo = lax.broadcasted_iota(jnp.int32, (tm, LANES), 1) < HEAD_DIM

        def col(c0, width):
            return _dot(xb, w_ref[:, c0:c0 + width])

        sbq_ref[0] = (col(0, WIDTH) * ATTN_SCALE).astype(BF16)
        h = col(WIDTH, WIDTH)
        sbk32_ref[0] = h
        sbk16_ref[0] = h.astype(BF16)
        h = col(2 * WIDTH, WIDTH)
        sbv32_ref[0] = h
        sbv16_ref[0] = h.astype(BF16)

        hq_all = col(4 * WIDTH, WIDTH)
        hk_all = col(5 * WIDTH, WIDTH)
        for s in range(HEAD_PAIRS):
            sl = slice(s * LANES, (s + 1) * LANES)
            hq = _head_rms_norm(hq_all[:, sl], qnw_ref[...], lane_lo)
            cbq_ref[0, :, sl] = (hq * (ATTN_SCALE * LOG2E)).astype(BF16)
            hk = _head_rms_norm(hk_all[:, sl], knw_ref[...], lane_lo)
            cbk16_ref[0, :, sl] = hk.astype(BF16)
            cbk32_ref[0, :, sl] = hk

        h = col(6 * WIDTH, WIDTH)
        cbv16_ref[0] = h.astype(BF16)
        cbv32_ref[0] = h


def _project(x, nw, w_bf, qnw128, knw128, *, tm, pad_rows, tail):
    b, t, _ = x.shape
    nt = t // tm
    pad_blocks = pad_rows // tm
    tail_blocks = tail // tm
    first_tail_step = pad_blocks + nt - tail_blocks

    def row(bi, j):
        return (bi, jnp.maximum(j - pad_blocks, 0), 0)

    def padded(bi, j):
        return (bi, j, 0)

    def tail_map(bi, j):
        return (bi, jnp.maximum(j - first_tail_step, 0), 0)

    def const(bi, j):
        return (0, 0)

    def out(width, dtype, rows=t):
        return jax.ShapeDtypeStruct((b, rows, width), dtype)

    blk = lambda width, imap: pl.BlockSpec((1, tm, width), imap)
    out_shape = [
        out(WIDTH, BF16), out(WIDTH, F32), out(WIDTH, BF16, t + pad_rows),
        out(WIDTH, F32), out(WIDTH, BF16, t + pad_rows),
        out(WIDTH, BF16), out(WIDTH, BF16, t + pad_rows), out(WIDTH, BF16, t + pad_rows),
        out(WIDTH, F32, tail), out(WIDTH, F32, tail),
    ]
    out_specs = [
        blk(WIDTH, row), blk(WIDTH, row), blk(WIDTH, padded),
        blk(WIDTH, row), blk(WIDTH, padded),
        blk(WIDTH, row), blk(WIDTH, padded), blk(WIDTH, padded),
        blk(WIDTH, tail_map), blk(WIDTH, tail_map),
    ]
    in_specs = [
        blk(D_MODEL, row),
        pl.BlockSpec((1, D_MODEL), const),
        pl.BlockSpec((D_MODEL, IN_COLS), const, pipeline_mode=pl.Buffered(1)),
        pl.BlockSpec((1, LANES), const),
        pl.BlockSpec((1, LANES), const),
    ]
    return pl.pallas_call(
        functools.partial(_proj_kernel, pad_blocks=pad_blocks),
        grid=(b, nt + pad_blocks),
        in_specs=in_specs, out_specs=out_specs, out_shape=out_shape,
        compiler_params=pltpu.CompilerParams(
            dimension_semantics=("arbitrary", "arbitrary"), vmem_limit_bytes=VMEM_LIMIT),
        name="project",
    )(x, nw, w_bf, qnw128, knw128)


def _sb_visit_items(scratch, uo, blk, jobs, n, causal):
    qm_all, acc_all, carry_all, z_all, sc_all = scratch
    sub = lambda u: slice(u * blk, (u + 1) * blk)
    masked = lambda u: causal is not None and u == n - 1

    def stage1(t, scores, h):
        z_all[t, h, :, :n * blk] = scores(h, qm_all[t, h])

    def stage2(t, h):
        zh = z_all[t, h, :, :n * blk]
        sp = jnp.maximum(zh, 0.0) + jnp.log(1.0 + jnp.exp2(jnp.abs(zh) * (-LOG2E)))
        for u in range(n):
            spu = jnp.where(causal, sp[:, sub(u)], 0.0) if masked(u) else sp[:, sub(u)]
            hi = spu.astype(BF16)
            lo = (spu - hi.astype(F32)).astype(BF16)
            sc_all[t, h, u] = _dot(jnp.concatenate([hi, lo], axis=1), uo)

    def stage3(t, weighted, h):
        carry = carry_all[t, h]
        ws = [None] * n
        for u in reversed(range(n)):
            w = jnp.exp(z_all[t, h, :, sub(u)] - sc_all[t, h, u, :, :blk] - carry)
            if masked(u):
                w = jnp.where(causal, w, 0.0)
            ws[u] = w.astype(BF16)
            carry = carry + sc_all[t, h, u, :, blk:]
        carry_all[t, h] = carry
        acc_all[t, h] += weighted(h, jnp.concatenate(ws, axis=1))

    heads = range(N_HEADS)
    return ([functools.partial(stage1, t, sc, h) for t, sc, _ in jobs for h in heads]
            + [functools.partial(stage2, t, h) for t, _, _ in jobs for h in heads]
            + [functools.partial(stage3, t, wt, h) for t, _, wt in jobs for h in heads])


def _sb_live(scratch, t):
    return jnp.min(scratch[2][t]) < -EXP_ZERO_BELOW


def _interleave(main, extra):
    if not extra:
        return list(main)
    every = max(len(main) // len(extra), 1)
    out, k = [], 0
    for n, item in enumerate(main):
        if n % every == 0 and k < len(extra):
            out.append(extra[k])
            k += 1
        out.append(item)
    return out + list(extra[k:])


def _sb_prompt_blocks(q_rows, k_rows, v_rows, store, scratch, uo, blk, blocks, new_pad,
                      extra=()):
    qm_all, acc_all, carry_all = scratch[:3]
    tiles = range(len(blocks))
    lane_lo = lax.broadcasted_iota(jnp.int32, (blk, LANES), 1) < HEAD_DIM
    causal = (lax.broadcasted_iota(jnp.int32, (blk, blk), 1)
              < lax.broadcasted_iota(jnp.int32, (blk, blk), 0))
    pair = lambda h: slice((h // 2) * LANES, (h // 2 + 1) * LANES)
    diags = [pl.multiple_of(new_pad + ib * blk, blk) for ib in blocks]
    for t in tiles:
        acc_all[t] = jnp.zeros(acc_all.shape[1:], F32)
        carry_all[t] = jnp.zeros(carry_all.shape[1:], F32)
        q = q_rows(t)
        for hp in range(HEAD_PAIRS):
            q2 = q[:, pair(2 * hp)]
            zero = jnp.zeros_like(q2)
            qm_all[t, 2 * hp] = jnp.where(lane_lo, q2, zero)
            qm_all[t, 2 * hp + 1] = jnp.where(lane_lo, zero, q2)

    def new_rows(t, s, n):
        k_tile = k_rows(s, n)
        v_tile = v_rows(s, n)
        return (t, lambda h, q: _dot_nt(q, k_tile[:, pair(h)]),
                lambda h, w: _dot(w, v_tile[:, pair(h)]))

    first = []
    for t in tiles:
        first += _sb_visit_items(
            scratch, uo, blk, [new_rows(t, pl.multiple_of(diags[t] - 2 * blk, blk), 3)], 3,
            causal)
    for item in _interleave(first, list(extra)):
        item()

    for t in tiles:
        def older_new(c, t=t):
            j, _ = c
            s = pl.multiple_of(diags[t] - (j + 1) * 2 * blk, blk)
            for item in _sb_visit_items(scratch, uo, blk, [new_rows(t, s, 2)], 2, None):
                item()
            return j + 1, _sb_live(scratch, t)

        def cond_new(c, t=t):
            j, live = c
            return jnp.logical_and(j < (blocks[t] + 1) // 2, live)
        lax.while_loop(cond_new, older_new, (jnp.int32(1), _sb_live(scratch, t)))

    for t in tiles:
        for hp in range(HEAD_PAIRS):
            store(t, hp, jnp.where(lane_lo, acc_all[t, 2 * hp], acc_all[t, 2 * hp + 1]))


def _sb_kernel(*refs, blk, qtiles, new_pad, n_cache):
    if n_cache:
        q_ref, kn_ref, vn_ref, kc_ref, vc_ref, uo_ref, o_ref = refs[:7]
    else:
        q_ref, kn_ref, vn_ref, uo_ref, o_ref = refs[:5]
    scratch = refs[-5:]
    qm_all, acc_all, carry_all = scratch[:3]
    uo = uo_ref[...]

    if new_pad is not None:
        rows_of = lambda t: slice(t * blk, (t + 1) * blk)

        def store(t, hp, value):
            o_ref[0, rows_of(t), hp * LANES:(hp + 1) * LANES] = value.astype(o_ref.dtype)

        _sb_prompt_blocks(lambda t: q_ref[0, rows_of(t), :],
                          lambda s, n: kn_ref[0, pl.ds(s, n * blk), :],
                          lambda s, n: vn_ref[0, pl.ds(s, n * blk), :],
                          store, scratch, uo, blk,
                          [pl.program_id(1) * qtiles + t for t in range(qtiles)], new_pad)
    else:
        head = lambda h: slice(h * HEAD_DIM, (h + 1) * HEAD_DIM)
        causal = (lax.broadcasted_iota(jnp.int32, (blk, blk), 1)
                  < lax.broadcasted_iota(jnp.int32, (blk, blk), 0))
        acc_all[...] = jnp.zeros(acc_all.shape, F32)
        carry_all[...] = jnp.zeros(carry_all.shape, F32)
        for h in range(N_HEADS):
            qm_all[0, h] = q_ref[0, :, head(h)]

        def visit(job, n, causal):
            for item in _sb_visit_items(scratch, uo, blk, [job], n, causal):
                item()
            return _sb_live(scratch, 0)

        def cache_cols(ref, h, s):
            return ref[0, h, :, pl.ds(s, 2 * blk)].astype(BF16)

        def cache_rows(s):
            return (lambda h, q: _dot(q, cache_cols(kc_ref, h, s)),
                    lambda h, w: _dot_nt(w, cache_cols(vc_ref, h, s)))

        top = (n_cache - 1) * 2 * blk
        old_scores, old_weighted = cache_rows(top)
        live = visit(
            (0,
             lambda h, q: jnp.concatenate(
                 [old_scores(h, q), _dot_nt(q, kn_ref[0, :, head(h)])], axis=1),
             lambda h, w: (old_weighted(h, w[:, :2 * blk])
                           + _dot(w[:, 2 * blk:], vn_ref[0, :, head(h)]))),
            3, causal)

        def older_cache(c):
            j, _ = c
            s = pl.multiple_of(j * 2 * blk, 2 * blk)
            return j - 1, visit((0,) + cache_rows(s), 2, None)

        def cond_cache(c):
            j, live = c
            return jnp.logical_and(j >= 0, live)
        lax.while_loop(cond_cache, older_cache, (jnp.int32(n_cache - 2), live))

        for h in range(N_HEADS):
            o_ref[0, :, head(h)] = acc_all[0, h].astype(o_ref.dtype)


def _proj_sb_kernel(x_ref, nw_ref, wsb_ref, wcbqk_ref, wcbv_ref, qnw_ref, knw_ref, uo_ref,
                    sbk32_ref, sbv32_ref, cbq_ref, cbk16_ref, cbv16_ref, cbk32_ref, cbv32_ref,
                    osb_ref, xb_ref, q_s, k_s, v_s, qt_s, kt_s, vt_s, *scratch,
                    blk, qtiles, tm, row_tiles, pad):
    s = pl.program_id(0)
    per_seq = row_tiles + 1
    j = s % per_seq
    uo = uo_ref[...]
    lane_lo = lax.broadcasted_iota(jnp.int32, (tm, LANES), 1) < HEAD_DIM

    def zero_pad_blocks():
        cbk16_ref[...] = jnp.zeros(cbk16_ref.shape, BF16)
        cbv16_ref[...] = jnp.zeros(cbv16_ref.shape, BF16)

    def project_items():
        def normalise():
            xb_ref[...] = _normed_bf16(x_ref[0], nw_ref[...])

        piece = 2 * LANES
        halves = range(0, WIDTH, piece)

        def col(w_ref, c0):
            return _dot(xb_ref[...], w_ref[:, c0:c0 + piece])

        def sb_q(c):
            qt_s[:, c:c + piece] = (col(wsb_ref, c) * ATTN_SCALE).astype(BF16)

        def sb_k(c):
            h = col(wsb_ref, WIDTH + c)
            sbk32_ref[0, :, c:c + piece] = h
            kt_s[:, c:c + piece] = h.astype(BF16)

        def sb_v(c):
            h = col(wsb_ref, 2 * WIDTH + c)
            sbv32_ref[0, :, c:c + piece] = h
            vt_s[:, c:c + piece] = h.astype(BF16)

        def cb_q(c):
            h = col(wcbqk_ref, c)
            for p in range(piece // LANES):
                sl = slice(c + p * LANES, c + (p + 1) * LANES)
                hq = _head_rms_norm(h[:, p * LANES:(p + 1) * LANES], qnw_ref[...], lane_lo)
                cbq_ref[0, :, sl] = (hq * (ATTN_SCALE * LOG2E)).astype(BF16)

        def cb_k(c):
            h = col(wcbqk_ref, WIDTH + c)
            for p in range(piece // LANES):
                sl = slice(c + p * LANES, c + (p + 1) * LANES)
                hk = _head_rms_norm(h[:, p * LANES:(p + 1) * LANES], knw_ref[...], lane_lo)
                cbk16_ref[0, :, sl] = hk.astype(BF16)
                cbk32_ref[0, :, sl] = hk

        def cb_v(c):
            h = col(wcbv_ref, c)
            cbv16_ref[0, :, c:c + piece] = h.astype(BF16)
            cbv32_ref[0, :, c:c + piece] = h

        pieces = lambda fn: [functools.partial(fn, c) for c in halves]
        return ([normalise] + pieces(sb_q) + pieces(sb_k) + pieces(sb_v),
                pieces(cb_q) + pieces(cb_k) + pieces(cb_v))

    def attend(prev_tile, extras):
        rows = pl.ds(pl.multiple_of(pad + prev_tile * tm, tm), tm)
        q_s[...] = qt_s[...]
        k_s[rows, :] = kt_s[...]
        v_s[rows, :] = vt_s[...]
        for g in range(tm // (blk * qtiles)):
            rows_of = lambda t, g=g: slice((g * qtiles + t) * blk, (g * qtiles + t + 1) * blk)

            def store(t, hp, value, rows_of=rows_of):
                osb_ref[0, rows_of(t), hp * LANES:(hp + 1) * LANES] = value.astype(osb_ref.dtype)

            _sb_prompt_blocks(
                lambda t, rows_of=rows_of: q_s[rows_of(t), :],
                lambda r, n: k_s[pl.ds(r, n * blk), :],
                lambda r, n: v_s[pl.ds(r, n * blk), :],
                store, scratch, uo, blk,
                [prev_tile * (tm // blk) + g * qtiles + t for t in range(qtiles)], pad,
                extra=extras[g] if g < len(extras) else ())
        for late in extras[tm // (blk * qtiles):]:
            for item in late:
                item()

    @pl.when(s == 0)
    def _():
        k_s[0:pad, :] = jnp.zeros((pad, WIDTH), BF16)
        v_s[0:pad, :] = jnp.zeros((pad, WIDTH), BF16)

    @pl.when(jnp.logical_and(j == 0, s < pl.num_programs(0) - 1))
    def _():
        zero_pad_blocks()

    @pl.when(jnp.logical_and(j == 0, s > 0))
    def _():
        attend(row_tiles - 1, [])

    @pl.when(j == 1)
    def _():
        first, second = project_items()
        for item in first + second:
            item()

    @pl.when(j >= 2)
    def _():
        attend(j - 2, list(project_items()))


def _project_sb(x, nw, w_bf, qnw128, knw128, *, tm, blk, pad):
    b, t, _ = x.shape
    n = t // tm
    per_seq = n + 1
    steps = b * per_seq + 1
    qtiles = 2
    tail = min(BAND, t)
    assert t % tm == 0 and pad == tm and tail == tm and tm % (blk * qtiles) == 0

    def last_projected(s):
        g = jnp.maximum(s - s // per_seq - 1, 0)
        return g // n, g % n

    def row(s):
        return last_projected(s) + (0,)

    def padded(s):
        c = jnp.minimum(s, steps - 2)
        return (c // per_seq, c % per_seq, 0)

    def tail_map(s):
        return (last_projected(s)[0], 0, 0)

    def attended(s):
        return last_projected(jnp.maximum(s - 1, 0)) + (0,)

    blkspec = lambda width, imap: pl.BlockSpec((1, tm, width), imap)
    uo = _cumsum_matrix(blk)
    out = lambda width, dtype, rows=t: jax.ShapeDtypeStruct((b, rows, width), dtype)
    out_shape = [out(WIDTH, F32), out(WIDTH, F32), out(WIDTH, BF16),
                 out(WIDTH, BF16, t + pad), out(WIDTH, BF16, t + pad),
                 out(WIDTH, F32, tail), out(WIDTH, F32, tail), out(WIDTH, BF16)]
    out_specs = [blkspec(WIDTH, row), blkspec(WIDTH, row), blkspec(WIDTH, row),
                 blkspec(WIDTH, padded), blkspec(WIDTH, padded),
                 blkspec(WIDTH, tail_map), blkspec(WIDTH, tail_map), blkspec(WIDTH, attended)]
    in_specs = [blkspec(D_MODEL, row),
                _resident((1, D_MODEL), (0, 0)),
                _resident((D_MODEL, 3 * WIDTH), (0, 0)),
                _resident((D_MODEL, 2 * WIDTH), (0, 2)),
                _resident((D_MODEL, WIDTH), (0, 6)),
                _resident((1, LANES), (0, 0)), _resident((1, LANES), (0, 0)),
                _resident(uo.shape, (0, 0))]
    d = LANES
    return pl.pallas_call(
        functools.partial(_proj_sb_kernel, blk=blk, qtiles=qtiles, tm=tm, row_tiles=n, pad=pad),
        grid=(steps,),
        in_specs=in_specs, out_specs=out_specs, out_shape=out_shape,
        scratch_shapes=[pltpu.VMEM((tm, D_MODEL), BF16),
                        pltpu.VMEM((tm, WIDTH), BF16),
                        pltpu.VMEM((pad + t, WIDTH), BF16),
                        pltpu.VMEM((pad + t, WIDTH), BF16),
                        pltpu.VMEM((tm, WIDTH), BF16),
                        pltpu.VMEM((tm, WIDTH), BF16),
                        pltpu.VMEM((tm, WIDTH), BF16),
                        pltpu.VMEM((qtiles, N_HEADS, blk, d), BF16),
                        pltpu.VMEM((qtiles, N_HEADS, blk, d), F32),
                        pltpu.VMEM((qtiles, N_HEADS, blk, blk), F32),
                        pltpu.VMEM((qtiles, N_HEADS, blk, 3 * blk), F32),
                        pltpu.VMEM((qtiles, N_HEADS, 3, blk, 2 * blk), F32)],
        compiler_params=pltpu.CompilerParams(
            dimension_semantics=("arbitrary",), vmem_limit_bytes=VMEM_LIMIT),
        name="project_sb",
    )(x, nw, w_bf, w_bf, w_bf, qnw128, knw128, uo)


def _cumsum_matrix(blk):
    j = np.arange(blk)[:, None]
    s = np.arange(blk)[None, :]
    u = (j >= s).astype(np.float32)
    half = np.concatenate([u, np.ones((blk, blk), np.float32)], axis=1)
    return jnp.asarray(np.concatenate([half, half], axis=0), dtype=BF16)


def _stick_breaking(q, k_new, v_new, k_cache=None, v_cache=None, *, blk):
    b, tq, _ = q.shape
    nq = tq // blk
    new_pad = k_new.shape[1] - tq if nq > 1 else None
    assert (new_pad is None) == (k_cache is not None)
    assert new_pad is None or new_pad >= 2 * blk
    n_cache = 0 if k_cache is None else k_cache.shape[-1] // (2 * blk)
    d = HEAD_DIM if n_cache else LANES
    full = lambda arr: pl.BlockSpec((1,) + arr.shape[1:],
                                    lambda bi, i: (bi,) + (0,) * (arr.ndim - 1))
    qtiles = 2 if nq % 2 == 0 else 1
    qblk = pl.BlockSpec((1, qtiles * blk, WIDTH), lambda bi, i: (bi, i, 0))
    uo = _cumsum_matrix(blk)
    operands = [q, k_new, v_new]
    in_specs = [qblk, full(k_new), full(v_new)]
    if n_cache:
        operands += [k_cache, v_cache]
        in_specs += [full(k_cache), full(v_cache)]
    operands.append(uo)
    in_specs.append(pl.BlockSpec(uo.shape, lambda bi, i: (0, 0)))
    return pl.pallas_call(
        functools.partial(_sb_kernel, blk=blk, qtiles=qtiles, new_pad=new_pad, n_cache=n_cache),
        grid=(b, nq // qtiles),
        in_specs=in_specs, out_specs=qblk,
        out_shape=jax.ShapeDtypeStruct((b, tq, WIDTH), BF16),
        scratch_shapes=[pltpu.VMEM((qtiles, N_HEADS, blk, d), BF16),
                        pltpu.VMEM((qtiles, N_HEADS, blk, d), F32),
                        pltpu.VMEM((qtiles, N_HEADS, blk, blk), F32),
                        pltpu.VMEM((qtiles, N_HEADS, blk, 3 * blk), F32),
                        pltpu.VMEM((qtiles, N_HEADS, 3, blk, 2 * blk), F32)],
        compiler_params=pltpu.CompilerParams(
            dimension_semantics=("arbitrary", "arbitrary"), vmem_limit_bytes=VMEM_LIMIT),
        name="stick_breaking",
    )(*operands)


def _bias_kernel(rb_ref, o_ref, *, g):
    rows, win = g * CHUNK, (LEFT_CHUNKS + g) * CHUNK
    n = 768
    npad = rb_ref.shape[1]
    jj = lax.broadcasted_iota(jnp.int32, (npad, n), 1)
    rr = lax.broadcasted_iota(jnp.int32, (npad, n), 0)
    idx = jnp.where(jj > win, 2 * REL_CLIP,
                    jnp.clip(BAND - jj, -REL_CLIP, REL_CLIP) + REL_CLIP)
    sel = (rr == idx).astype(BF16)
    rb = rb_ref[...]
    hi = rb.astype(BF16)
    r1 = rb - hi.astype(F32)
    mid = r1.astype(BF16)
    lo = (r1 - mid.astype(F32)).astype(BF16)
    line = _dot(hi, sel) + _dot(mid, sel) + _dot(lo, sel)
    line = line * LOG2E
    tq = lax.broadcasted_iota(jnp.int32, (rows, win), 0)
    tk = lax.broadcasted_iota(jnp.int32, (rows, win), 1)
    off = tk - (tq // CHUNK) * CHUNK
    valid = jnp.logical_and(off >= 0, off < (LEFT_CHUNKS + 1) * CHUNK)
    for h in range(N_HEADS):
        tiled = jnp.broadcast_to(line[h:h + 1, :], (rows, n))
        rolled = pltpu.roll(tiled, 0, 1, stride=1, stride_axis=0)
        o_ref[h] = jnp.where(valid, rolled[:, :win], NEG_INF)


def _bias_table(rb_pad, g):
    rows, win = g * CHUNK, (LEFT_CHUNKS + g) * CHUNK
    return pl.pallas_call(
        functools.partial(_bias_kernel, g=g),
        out_shape=jax.ShapeDtypeStruct((N_HEADS, rows, win), F32),
        name="bias_table",
    )(rb_pad)


def _band_tiles(q_ref, k_ref, v_ref, bias_ref, store, s_refs, m_refs, zero_idx, *,
                rows, win, tile_rows, tile_starts, mask_pad):
    lane_lo = lax.broadcasted_iota(jnp.int32, (rows, LANES), 1) < HEAD_DIM
    work = [(t, hp) for t in range(len(tile_rows)) for hp in range(HEAD_PAIRS)]

    def first(t, hp):
        sl = slice(hp * LANES, (hp + 1) * LANES)
        q2 = q_ref[0, tile_rows[t], sl]
        zero = jnp.zeros_like(q2)
        qm = jnp.concatenate([jnp.where(lane_lo, q2, zero), jnp.where(lane_lo, zero, q2)],
                             axis=0)
        s = _dot_nt(qm, k_ref[0, pl.ds(tile_starts[t], win), sl]) + bias_ref[hp]
        if mask_pad:
            key_ok = (lax.broadcasted_iota(jnp.int32, (2 * rows, win), 1)
                      >= BAND - tile_starts[t])
            s = jnp.where(key_ok, s, NEG_INF)
        s_refs[t * HEAD_PAIRS + hp][0] = s
        m_refs[t * HEAD_PAIRS + hp][0] = jnp.broadcast_to(
            jnp.max(s, axis=-1, keepdims=True), (2 * rows, LANES))

    def second(t, hp):
        sl = slice(hp * LANES, (hp + 1) * LANES)
        s = s_refs[t * HEAD_PAIRS + hp][zero_idx]
        m = m_refs[t * HEAD_PAIRS + hp][zero_idx]
        m = jnp.concatenate([m] * (win // LANES), axis=1) if win % LANES == 0 else m[:, :1]
        e = jnp.exp2(s - m)
        l = jnp.sum(e, axis=-1, keepdims=True)
        o = _dot(e.astype(BF16), v_ref[0, pl.ds(tile_starts[t], win), sl]) / l
        store(t, hp, jnp.where(lane_lo, o[:rows], o[rows:]))

    return ([functools.partial(first, t, hp) for t, hp in work]
            + [functools.partial(second, t, hp) for t, hp in work])


def _cb_kernel(q_ref, k_ref, v_ref, bias_ref, o_ref, *scratch, rows, win, tiles, pad_history):
    n_slots = tiles * HEAD_PAIRS
    s_refs, m_refs = scratch[:n_slots], scratch[n_slots:]
    p = pl.program_id(1)
    tile_rows = [slice(t * rows, (t + 1) * rows) for t in range(tiles)]

    def store(t, hp, value):
        o_ref[0, tile_rows[t], hp * LANES:(hp + 1) * LANES] = value.astype(o_ref.dtype)

    def body(mask_pad):
        starts = [pl.multiple_of((p * tiles + t) * rows, rows) for t in range(tiles)]
        for item in _band_tiles(q_ref, k_ref, v_ref, bias_ref, store, s_refs, m_refs,
                                jnp.minimum(p, 0), rows=rows, win=win, tile_rows=tile_rows,
                                tile_starts=starts, mask_pad=mask_pad):
            item()

    if pad_history:
        pad_steps = BAND // (rows * tiles)
        pl.when(p < pad_steps)(functools.partial(body, True))
        pl.when(p >= pad_steps)(functools.partial(body, False))
    else:
        body(False)


def _chunk_band(q, k_hist, v_hist, bias, *, g, tiles, pad_history):
    b, t, _ = q.shape
    rows, win = g * CHUNK, (LEFT_CHUNKS + g) * CHUNK
    assert BAND % (rows * tiles) == 0 and t % (rows * tiles) == 0
    full = lambda arr: pl.BlockSpec((1,) + arr.shape[1:], lambda bi, i: (bi, 0, 0))
    qblk = pl.BlockSpec((1, tiles * rows, WIDTH), lambda bi, i: (bi, i, 0))
    bias = bias.reshape(HEAD_PAIRS, 2 * rows, win)
    return pl.pallas_call(
        functools.partial(_cb_kernel, rows=rows, win=win, tiles=tiles, pad_history=pad_history),
        grid=(b, t // (rows * tiles)),
        in_specs=[qblk, full(k_hist), full(v_hist),
                  pl.BlockSpec(bias.shape, lambda bi, i: (0, 0, 0))],
        out_specs=qblk,
        out_shape=jax.ShapeDtypeStruct((b, t, WIDTH), BF16),
        scratch_shapes=([pltpu.VMEM((1, 2 * rows, win), F32)] * (tiles * HEAD_PAIRS)
                        + [pltpu.VMEM((1, 2 * rows, LANES), F32)] * (tiles * HEAD_PAIRS)),
        compiler_params=pltpu.CompilerParams(
            dimension_semantics=("arbitrary", "arbitrary"), vmem_limit_bytes=VMEM_LIMIT),
        name="chunk_band",
    )(q, k_hist, v_hist, bias)


def _merge_rows(x, osb, ocb, nw, wzsb, wzcb, wg, wsb, wcb, wout):
    xb = _normed_bf16(x, nw)
    zsb = _dot(xb, wzsb)
    zcb = _dot(xb, wzcb)
    a_sb = (osb.astype(F32) * (zsb * _sigmoid(zsb))).astype(BF16)
    a_cb = (ocb.astype(F32) * (zcb * _sigmoid(zcb))).astype(BF16)
    b_sb = _dot(a_sb, wsb)
    b_cb = _dot(a_cb, wcb)
    g = _dot(xb, wg)
    h = _sigmoid(g[:, :D_MODEL]) * b_sb + _sigmoid(g[:, D_MODEL:]) * b_cb
    return x + _dot(h.astype(BF16), wout)


def _merge_kernel(x_ref, osb_ref, ocb_ref, nw_ref, wzsb_ref, wzcb_ref, wg_ref,
                  wsb_ref, wcb_ref, wout_ref, y_ref):
    y_ref[0] = _merge_rows(x_ref[0], osb_ref[0], ocb_ref[0], nw_ref[...], wzsb_ref[...],
                           wzcb_ref[...], wg_ref[...], wsb_ref[...], wcb_ref[...],
                           wout_ref[...])


def _resident(shape, index):
    return pl.BlockSpec(shape, lambda *_: index, pipeline_mode=pl.Buffered(1))


def _merge_weight_specs():
    return [_resident((1, D_MODEL), (0, 0)),
            _resident((D_MODEL, WIDTH), (0, 3)), _resident((D_MODEL, WIDTH), (0, 7)),
            _resident((D_MODEL, 2 * D_MODEL), (0, 2)),
            _resident((WIDTH, D_MODEL), (0, 0)), _resident((WIDTH, D_MODEL), (0, 0)),
            _resident((D_MODEL, D_MODEL), (0, 0))]


def _merge(x, osb, ocb, nw, w_bf, wsb, wcb, wout, *, tm):
    b, t, _ = x.shape
    wide = pl.BlockSpec((1, tm, D_MODEL), lambda bi, i: (bi, i, 0))
    narrow = pl.BlockSpec((1, tm, WIDTH), lambda bi, i: (bi, i, 0))
    return pl.pallas_call(
        _merge_kernel,
        grid=(b, t // tm),
        in_specs=[wide, narrow, narrow] + _merge_weight_specs(),
        out_specs=wide,
        out_shape=jax.ShapeDtypeStruct((b, t, D_MODEL), F32),
        compiler_params=pltpu.CompilerParams(
            dimension_semantics=("arbitrary", "arbitrary"), vmem_limit_bytes=VMEM_LIMIT),
        name="merge",
    )(x, osb, ocb, nw, w_bf, w_bf, w_bf, wsb, wcb, wout)


def _band_merge_kernel(q_ref, k_ref, v_ref, bias_ref, x_ref, osb_ref, nw_ref, wzsb_ref,
                       wzcb_ref, wg_ref, wsb_ref, wcb_ref, wout_ref, y_ref, ocb_ref, *scratch,
                       rows, win, group, tiles_per_row_tile, row_tiles):
    n_slots = group * HEAD_PAIRS
    xb_ref, zg_ref = scratch[:2]
    s_refs, m_refs = scratch[2:2 + n_slots], scratch[2 + n_slots:]
    s = pl.program_id(0)
    slot = s % 2
    i = jnp.minimum(s, pl.num_programs(0) - 2) % row_tiles

    @pl.when(s == 0)
    def _():
        ocb_ref[...] = jnp.zeros(ocb_ref.shape, ocb_ref.dtype)

    def normalise():
        xb_ref[...] = _normed_bf16(x_ref[0], nw_ref[...])

    def gate_cols(c0, width, w_ref, w0):
        def run():
            zg_ref[:, c0:c0 + width] = _dot(xb_ref[...], w_ref[:, w0:w0 + width])
        return run

    def finish():
        zsb = zg_ref[:, :WIDTH]
        zcb = zg_ref[:, WIDTH:2 * WIDTH]
        a_sb = (osb_ref[0].astype(F32) * (zsb * _sigmoid(zsb))).astype(BF16)
        a_cb = (ocb_ref[1 - slot].astype(F32) * (zcb * _sigmoid(zcb))).astype(BF16)
        h = (_sigmoid(zg_ref[:, 2 * WIDTH:2 * WIDTH + D_MODEL]) * _dot(a_sb, wsb_ref[...])
             + _sigmoid(zg_ref[:, 2 * WIDTH + D_MODEL:]) * _dot(a_cb, wcb_ref[...]))
        y_ref[0] = x_ref[0] + _dot(h.astype(BF16), wout_ref[...])

    piece = 2 * LANES
    merge_items = ([normalise]
                   + [gate_cols(c, piece, wzsb_ref, c) for c in range(0, WIDTH, piece)]
                   + [gate_cols(WIDTH + c, piece, wzcb_ref, c) for c in range(0, WIDTH, piece)]
                   + [gate_cols(2 * WIDTH + c, piece, wg_ref, c)
                      for c in range(0, 2 * D_MODEL, piece)]
                   + [finish])

    def body(mask_pad):
        band_items = []
        for g0 in range(0, tiles_per_row_tile, group):
            tile_rows = [slice((g0 + t) * rows, (g0 + t + 1) * rows) for t in range(group)]
            starts = [pl.multiple_of((i * tiles_per_row_tile + g0 + t) * rows, rows)
                      for t in range(group)]

            def store(t, hp, value, tile_rows=tile_rows):
                ocb_ref[slot, tile_rows[t], hp * LANES:(hp + 1) * LANES] = value.astype(
                    ocb_ref.dtype)

            band_items += _band_tiles(q_ref, k_ref, v_ref, bias_ref, store, s_refs, m_refs,
                                      jnp.minimum(s, 0), rows=rows, win=win,
                                      tile_rows=tile_rows, tile_starts=starts,
                                      mask_pad=mask_pad)
        every = len(band_items) // len(merge_items)
        for n, item in enumerate(band_items):
            if n % every == 0 and n // every < len(merge_items):
                merge_items[n // every]()
            item()

    pl.when(i == 0)(functools.partial(body, True))
    pl.when(i != 0)(functools.partial(body, False))


def _band_merge(q, k_hist, v_hist, bias, x, osb, nw, w_bf, wsb, wcb, wout, *, g, group, tm):
    b, t, _ = q.shape
    rows, win = g * CHUNK, (LEFT_CHUNKS + g) * CHUNK
    row_tiles = t // tm
    tiles_per_row_tile = tm // rows
    assert t % tm == 0 and tm % (rows * group) == 0 and BAND <= tm
    steps = b * row_tiles + 1
    last = b * row_tiles - 1

    def cur(s):
        c = jnp.minimum(s, last)
        return c // row_tiles, c % row_tiles

    def prev(s):
        c = jnp.maximum(s - 1, 0)
        return c // row_tiles, c % row_tiles

    q_spec = pl.BlockSpec((1, tm, WIDTH), lambda s: cur(s) + (0,))
    hist = pl.BlockSpec((1,) + k_hist.shape[1:], lambda s: (cur(s)[0], 0, 0),
                        pipeline_mode=pl.Buffered(1))
    wide = pl.BlockSpec((1, tm, D_MODEL), lambda s: prev(s) + (0,))
    narrow = pl.BlockSpec((1, tm, WIDTH), lambda s: prev(s) + (0,))
    bias = bias.reshape(HEAD_PAIRS, 2 * rows, win)
    return pl.pallas_call(
        functools.partial(_band_merge_kernel, rows=rows, win=win, group=group,
                          tiles_per_row_tile=tiles_per_row_tile, row_tiles=row_tiles),
        grid=(steps,),
        in_specs=[q_spec, hist, hist, _resident(bias.shape, (0, 0, 0)), wide, narrow]
        + _merge_weight_specs(),
        out_specs=wide,
        out_shape=jax.ShapeDtypeStruct((b, t, D_MODEL), F32),
        scratch_shapes=([pltpu.VMEM((2, tm, WIDTH), BF16),
                         pltpu.VMEM((tm, D_MODEL), BF16),
                         pltpu.VMEM((tm, 2 * WIDTH + 2 * D_MODEL), F32)]
                        + [pltpu.VMEM((1, 2 * rows, win), F32)] * (group * HEAD_PAIRS)
                        + [pltpu.VMEM((1, 2 * rows, LANES), F32)] * (group * HEAD_PAIRS)),
        compiler_params=pltpu.CompilerParams(
            dimension_semantics=("arbitrary",), vmem_limit_bytes=VMEM_LIMIT),
        name="band_merge",
    )(q, k_hist, v_hist, bias, x, osb, nw, w_bf, w_bf, w_bf, wsb, wcb, wout)


def _heads(a):
    b, t, _ = a.shape
    return a.reshape(b, t, N_HEADS, HEAD_DIM)


def kernel(x_prompt, x_sample, cache_sb_k, cache_sb_v, cache_cb_k, cache_cb_v, norm_w, w_in,
           q_norm_w, k_norm_w, rel_bias, w_proj_sb, w_proj_cb, w_out):
    depth = w_in.shape[0]
    b, t, _ = x_prompt.shape
    bs, tn, _ = x_sample.shape
    p = cache_sb_k.shape[2]
    r = cache_cb_k.shape[2]
    assert t % (2 * CHUNK) == 0 and tn == CHUNK and r == BAND and p % (2 * tn) == 0

    y_p, y_s = x_prompt, x_sample
    outs = [[] for _ in range(8)]
    for l in range(depth):
        nw = norm_w[l].reshape(1, D_MODEL)
        w_bf = w_in[l].astype(BF16)
        qnw = jnp.tile(q_norm_w[l], 2).reshape(1, LANES)
        knw = jnp.tile(k_norm_w[l], 2).reshape(1, LANES)
        rb_pad = jnp.pad(rel_bias[l], ((0, 0), (0, 384 - (2 * REL_CLIP + 1))))
        bias2 = _bias_table(rb_pad, 2)
        bias1 = bias2[:, :CHUNK, :(LEFT_CHUNKS + 1) * CHUNK]
        wsb = w_proj_sb[l].astype(BF16)
        wcb = w_proj_cb[l].astype(BF16)
        wout = w_out[l].astype(BF16)

        (sbk32, sbv32, cbq, cbk16, cbv16, cbk32, cbv32, o_sb) = _project_sb(
            y_p, nw, w_bf, qnw, knw, tm=512, blk=128, pad=BAND)
        y_p_next = _band_merge(cbq, cbk16, cbv16, bias2, y_p, o_sb, nw, w_bf, wsb, wcb, wout,
                               g=2, group=2, tm=512)
        for lst, a in zip(outs[:4], (sbk32, sbv32, cbk32, cbv32)):
            lst.append(_heads(a))

        proj = _project(y_s.reshape(1, bs * tn, D_MODEL), nw, w_bf, qnw, knw,
                        tm=bs * tn, pad_rows=0, tail=bs * tn)
        (sbq, sbk32, sbk16, sbv32, sbv16, cbq, cbk16, cbv16, cbk32, cbv32) = [
            a.reshape(bs, tn, a.shape[-1]) for a in proj]
        o_sb = _stick_breaking(sbq, sbk16, sbv16,
                               jnp.transpose(cache_sb_k[l], (0, 2, 3, 1)),
                               jnp.transpose(cache_sb_v[l], (0, 2, 3, 1)), blk=tn)
        k_hist = jnp.concatenate([cache_cb_k[l].reshape(bs, r, WIDTH).astype(BF16), cbk16], axis=1)
        v_hist = jnp.concatenate([cache_cb_v[l].reshape(bs, r, WIDTH).astype(BF16), cbv16], axis=1)
        o_cb = _chunk_band(cbq, k_hist, v_hist, bias1, g=1, tiles=1, pad_history=False)
        flat = lambda a: a.reshape(1, bs * tn, a.shape[-1])
        y_s_next = _merge(flat(y_s), flat(o_sb), flat(o_cb), nw, w_bf, wsb, wcb, wout,
                          tm=bs * tn).reshape(bs, tn, D_MODEL)
        for lst, a in zip(outs[4:], (sbk32, sbv32, cbk32, cbv32)):
            lst.append(_heads(a))

        y_p, y_s = y_p_next, y_s_next

    return (y_p, y_s) + tuple(jnp.stack(o) for o in outs)
```
